```python
import math
import jax, jax.numpy as jnp
from jax import lax
import numpy as np

D_MODEL = 1024
BATCH = 16
SEQ = 2048
DEPTH = 4

N_MIXERS = 4
D_FF = 2816
NORM_EPS = 1e-6
GN_EPS = 1e-5
ROPE_THETA = 10000.0
Q_BLOCK = 128
NEG_INF = -1e30

A_HEADS = 16
A_HEAD_DIM = 64
A_PATTERNS = ((128, 1), (512, 4), (2048, 16))
A_GROUPS = len(A_PATTERNS)
A_WIDTH = A_GROUPS * A_HEADS * A_HEAD_DIM

B_HEADS = 16
B_NOPE = 64
B_ROPE = 32
B_QK = B_NOPE + B_ROPE
B_VDIM = 64
B_Q_RANK = 384
B_KV_RANK = 256

C_HEADS = 8
C_KDIM = 128
C_VDIM = 256
C_CHUNK = 128

D_WIDTH = 1024
D_BLOCKS = 4
D_BLOCK = D_WIDTH // D_BLOCKS
D_CONV = 4
LRU_C = 8.0

kernel_name = 'hybrid_interleaved_dilated_mla_retention_rglru'

f32 = jnp.float32


def rmsnorm(x, g):
    xf = x.astype(f32)
    y = xf * lax.rsqrt(jnp.mean(xf * xf, axis=-1, keepdims=True) + NORM_EPS)
    return (y * g.astype(f32)).astype(x.dtype)


def rope(x, pos):
    d = x.shape[-1]
    half = d // 2
    inv = ROPE_THETA ** (-jnp.arange(half, dtype=f32) * 2.0 / d)
    ang = pos.astype(f32)[:, :, None, None] * inv
    cos, sin = jnp.cos(ang), jnp.sin(ang)
    xf = x.astype(f32)
    x1, x2 = xf[..., :half], xf[..., half:]
    return jnp.concatenate([x1 * cos - x2 * sin, x2 * cos + x1 * sin], axis=-1).astype(x.dtype)


def swiglu(x, w_in, w_out):
    gate, up = jnp.split(x @ w_in, 2, axis=-1)
    return (jax.nn.silu(gate) * up) @ w_out


def _dilated_band(q, k, v, window, dilation):
    B, S, H, dh = q.shape
    W = window // dilation
    L = S // dilation
    nb = -(-L // W)
    Lp = nb * W

    def by_residue(t):
        t = t.reshape(B, L, dilation, H, t.shape[-1]).transpose(0, 2, 3, 1, 4)
        return jnp.pad(t, ((0, 0), (0, 0), (0, 0), (0, Lp - L), (0, 0)))

    def band(t):
        t = jnp.pad(t, ((0, 0), (0, 0), (0, 0), (W, 0), (0, 0)))
        t = t.reshape(B, dilation, H, nb + 1, W, t.shape[-1])
        return jnp.concatenate([t[:, :, :, :-1], t[:, :, :, 1:]], axis=4)

    qb = by_residue(q).reshape(B, dilation, H, nb, W, dh)
    kb = band(by_residue(k))
    vb = band(by_residue(v))
    s = jnp.einsum('brhnqd,brhnkd->brhnqk', qb, kb).astype(f32) * (dh ** -0.5)
    u = jnp.arange(W)[None, :, None]
    c = jnp.arange(2 * W)[None, None, :]
    blk = jnp.arange(nb)[:, None, None]
    valid = (c >= u) & (c <= u + W) & (blk * W - W + c >= 0)
    s = jnp.where(valid, s, NEG_INF)
    lse = jax.nn.logsumexp(s, axis=-1)
    p = jnp.exp(s - lse[..., None]).astype(v.dtype)
    o = jnp.einsum('brhnqk,brhnkd->brhnqd', p, vb)
    o = o.reshape(B, dilation, H, Lp, dh)[:, :, :, :L].transpose(0, 3, 1, 2, 4).reshape(B, S, H, dh)
    lse = lse.reshape(B, dilation, H, Lp)[..., :L].transpose(0, 3, 1, 2).reshape(B, S, H)
    return o, lse


def dilated_window_attention(x, pos, w_in, q_gain, k_gain, w_out):
    B, S, _ = x.shape
    q, k, v = jnp.split(x @ w_in, 3, axis=-1)
    shp = (B, S, A_GROUPS * A_HEADS, A_HEAD_DIM)
    q = rope(rmsnorm(q.reshape(shp), q_gain), pos)
    k = rope(rmsnorm(k.reshape(shp), k_gain), pos)
    v = v.reshape(shp)
    outs, lses = [], []
    for g, (window, dilation) in enumerate(A_PATTERNS):
        sl = slice(g * A_HEADS, (g + 1) * A_HEADS)
        o, lse = _dilated_band(q[:, :, sl], k[:, :, sl], v[:, :, sl], window, dilation)
        outs.append(o)
        lses.append(lse)
    wts = jax.nn.softmax(jnp.stack(lses, axis=0), axis=0).astype(x.dtype)
    o = jnp.einsum('gbsh,gbshd->bshd', wts, jnp.stack(outs, axis=0))
    return o.reshape(B, S, A_HEADS * A_HEAD_DIM) @ w_out


def _causal_attention(q, k, v):
    B, S, H, dq = q.shape
    nb = S // Q_BLOCK
    scale = dq ** -0.5
    qb = q.reshape(B, nb, Q_BLOCK, H, dq).transpose(1, 0, 2, 3, 4)
    kpos = jnp.arange(S)

    def block(args):
        qblk, b = args
        s = jnp.einsum('bqhd,bkhd->bhqk', qblk, k).astype(f32) * scale
        qpos = b * Q_BLOCK + jnp.arange(Q_BLOCK)
        s = jnp.where(kpos[None, :] <= qpos[:, None], s, NEG_INF)
        p = jax.nn.softmax(s, axis=-1).astype(v.dtype)
        return jnp.einsum('bhqk,bkhd->bqhd', p, v)

    o = lax.map(block, (qb, jnp.arange(nb)))
    return o.transpose(1, 0, 2, 3, 4).reshape(B, S, H, v.shape[-1])


def latent_attention(x, pos, w_in, q_a_gain, w_q_up, kv_a_gain, w_kv_up, q_gain, k_gain, w_out):
    B, S, _ = x.shape
    h = x @ w_in
    c_q = h[..., :B_Q_RANK]
    c_kv = h[..., B_Q_RANK:B_Q_RANK + B_KV_RANK]
    k_rope = h[..., B_Q_RANK + B_KV_RANK:]
    q = (rmsnorm(c_q, q_a_gain) @ w_q_up).reshape(B, S, B_HEADS, B_QK)
    kv = (rmsnorm(c_kv, kv_a_gain) @ w_kv_up).reshape(B, S, B_HEADS, B_NOPE + B_VDIM)
    k_nope, v = kv[..., :B_NOPE], kv[..., B_NOPE:]
    k = jnp.concatenate([k_nope, jnp.broadcast_to(k_rope[:, :, None, :], (B, S, B_HEADS, B_ROPE))], axis=-1)
    q = rmsnorm(q, q_gain)
    k = rmsnorm(k, k_gain)
    q = jnp.concatenate([q[..., :B_NOPE], rope(q[..., B_NOPE:], pos)], axis=-1)
    k = jnp.concatenate([k[..., :B_NOPE], rope(k[..., B_NOPE:], pos)], axis=-1)
    o = _causal_attention(q, k, v)
    return o.reshape(B, S, B_HEADS * B_VDIM) @ w_out


def retention(x, pos, w_in, w_out):
    B, S, _ = x.shape
    h = x @ w_in
    qk_w = C_HEADS * C_KDIM
    v_w = C_HEADS * C_VDIM
    q = h[..., :qk_w].reshape(B, S, C_HEADS, C_KDIM)
    k = h[..., qk_w:2 * qk_w].reshape(B, S, C_HEADS, C_KDIM)
    v = h[..., 2 * qk_w:2 * qk_w + v_w].reshape(B, S, C_HEADS, C_VDIM)
    g = h[..., 2 * qk_w + v_w:]
    q = rope(q, pos).astype(f32)
    k = rope(k, pos).astype(f32) * (C_KDIM ** -0.5)
    v = v.astype(f32)
    log_g = jnp.log(1.0 - 2.0 ** (-5.0 - jnp.arange(C_HEADS, dtype=f32)))
    idx = jnp.arange(C_CHUNK, dtype=f32)
    diff = idx[:, None] - idx[None, :]
    causal = diff >= 0
    decay = jnp.where(causal[None], jnp.exp(jnp.where(causal, diff, 0.0)[None] * log_g[:, None, None]), 0.0)
    xi = jnp.exp((idx + 1.0)[None, :] * log_g[:, None])
    zeta = jnp.exp((C_CHUNK - 1.0 - idx)[None, :] * log_g[:, None])
    chunk_decay = jnp.exp(C_CHUNK * log_g)
    nc = S // C_CHUNK

    def to_chunks(t):
        return t.reshape(B, nc, C_CHUNK, C_HEADS, t.shape[-1]).transpose(1, 0, 3, 2, 4)

    def step(R, inp):
        qc, kc, vc = inp
        s = jnp.einsum('bhid,bhjd->bhij', qc, kc) * decay[None]
        inner = jnp.einsum('bhij,bhjv->bhiv', s, vc)
        cross = jnp.einsum('bhid,bhdv->bhiv', qc, R) * xi[None, :, :, None]
        R = chunk_decay[None, :, None, None] * R + jnp.einsum('bhjd,bhjv->bhdv', kc * zeta[None, :, :, None], vc)
        return R, inner + cross

    R0 = jnp.zeros((B, C_HEADS, C_KDIM, C_VDIM), f32)
    _, y = lax.scan(step, R0, (to_chunks(q), to_chunks(k), to_chunks(v)))
    y = y.transpose(1, 0, 3, 2, 4).reshape(B, S, C_HEADS, C_VDIM)
    mu = jnp.mean(y, axis=-1, keepdims=True)
    var = jnp.mean(jnp.square(y - mu), axis=-1, keepdims=True)
    y = ((y - mu) * lax.rsqrt(var + GN_EPS)).astype(x.dtype).reshape(B, S, v_w)
    return (jax.nn.silu(g) * y) @ w_out


def _linear_combine(c1, c2):
    a1, b1 = c1
    a2, b2 = c2
    return a1 * a2, a2 * b1 + b2


def recurrent_block(x, w_in, conv_w, conv_b, w_rg, b_rg, w_ig, b_ig, lru_param, w_out):
    B, S, _ = x.shape
    gate, u = jnp.split(x @ w_in, 2, axis=-1)
    u = lax.conv_general_dilated(u, conv_w[:, None, :], window_strides=(1,),
                                 padding=[(D_CONV - 1, 0)],
                                 dimension_numbers=('NWC', 'WIO', 'NWC'),
                                 feature_group_count=D_WIDTH) + conv_b
    ub = u.reshape(B, S, D_BLOCKS, D_BLOCK)
    r = jax.nn.sigmoid(jnp.einsum('bsnc,ncd->bsnd', ub, w_rg).reshape(B, S, D_WIDTH) + b_rg)
    ig = jax.nn.sigmoid(jnp.einsum('bsnc,ncd->bsnd', ub, w_ig).reshape(B, S, D_WIDTH) + b_ig)
    log_a = -LRU_C * r.astype(f32) * jax.nn.softplus(-lru_param.astype(f32))
    a = jnp.exp(log_a)
    b = jnp.sqrt(jnp.maximum(-jnp.expm1(2.0 * log_a), 0.0)) * (ig * u).astype(f32)
    _, hs = lax.associative_scan(_linear_combine, (a, b), axis=1)
    return (jax.nn.gelu(gate) * hs.astype(x.dtype)) @ w_out


def _layers_of(m):
    return len(range(m, DEPTH, N_MIXERS))


def setup_inputs(seed: int = 0) -> dict:
    key = jax.random.key(seed)
    kit = iter(jax.random.split(key, 40))

    def w(shape, fan_in):
        return jax.random.normal(next(kit), shape, f32) * (fan_in ** -0.5)

    def gain(shape):
        return 1.0 + 0.05 * jax.random.normal(next(kit), shape, f32)

    def bias(shape):
        return 0.01 * jax.random.normal(next(kit), shape, f32)

    nA, nB, nC, nD = (_layers_of(m) for m in range(N_MIXERS))
    x = jax.random.normal(next(kit), (BATCH, SEQ, D_MODEL), f32)
    offsets = jax.random.randint(next(kit), (BATCH, 1), 0, 1024)
    positions = (offsets + jnp.arange(SEQ)[None, :]).astype(jnp.int32)
    u = jax.random.uniform(next(kit), (nD, D_WIDTH), f32, minval=0.9, maxval=0.999)
    a0 = u ** (1.0 / LRU_C)
    lru_param = jnp.log(a0) - jnp.log1p(-a0)
    return {
        'x': x,
        'positions': positions,
        'norm_gains': gain((DEPTH, 3, D_MODEL)),
        'ffn_w_in': w((DEPTH, 2, D_MODEL, 2 * D_FF), D_MODEL),
        'ffn_w_out': w((DEPTH, 2, D_FF, D_MODEL), D_FF),
        'a_w_in': w((nA, D_MODEL, 3 * A_WIDTH), D_MODEL),
        'a_q_gain': gain((nA, A_HEAD_DIM)),
        'a_k_gain': gain((nA, A_HEAD_DIM)),
        'a_w_out': w((nA, A_HEADS * A_HEAD_DIM, D_MODEL), A_HEADS * A_HEAD_DIM),
        'b_w_in': w((nB, D_MODEL, B_Q_RANK + B_KV_RANK + B_ROPE), D_MODEL),
        'b_q_a_gain': gain((nB, B_Q_RANK)),
        'b_w_q_up': w((nB, B_Q_RANK, B_HEADS * B_QK), B_Q_RANK),
        'b_kv_a_gain': gain((nB, B_KV_RANK)),
        'b_w_kv_up': w((nB, B_KV_RANK, B_HEADS * (B_NOPE + B_VDIM)), B_KV_RANK),
        'b_q_gain': gain((nB, B_QK)),
        'b_k_gain': gain((nB, B_QK)),
        'b_w_out': w((nB, B_HEADS * B_VDIM, D_MODEL), B_HEADS * B_VDIM),
        'c_w_in': w((nC, D_MODEL, 2 * C_HEADS * C_KDIM + 2 * C_HEADS * C_VDIM), D_MODEL),
        'c_w_out': w((nC, C_HEADS * C_VDIM, D_MODEL), C_HEADS * C_VDIM),
        'd_w_in': w((nD, D_MODEL, 2 * D_WIDTH), D_MODEL),
        'd_conv_w': w((nD, D_CONV, D_WIDTH), D_CONV),
        'd_conv_b': bias((nD, D_WIDTH)),
        'd_w_rg': w((nD, D_BLOCKS, D_BLOCK, D_BLOCK), D_BLOCK),
        'd_b_rg': bias((nD, D_WIDTH)),
        'd_w_ig': w((nD, D_BLOCKS, D_BLOCK, D_BLOCK), D_BLOCK),
        'd_b_ig': bias((nD, D_WIDTH)),
        'd_lru_param': lru_param,
        'd_w_out': w((nD, D_WIDTH, D_MODEL), D_WIDTH),
    }


def reference(x, positions, norm_gains, ffn_w_in, ffn_w_out,
              a_w_in, a_q_gain, a_k_gain, a_w_out,
              b_w_in, b_q_a_gain, b_w_q_up, b_kv_a_gain, b_w_kv_up, b_q_gain, b_k_gain, b_w_out,
              c_w_in, c_w_out,
              d_w_in, d_conv_w, d_conv_b, d_w_rg, d_b_rg, d_w_ig, d_b_ig, d_lru_param, d_w_out):
    for i in range(DEPTH):
        m, j = i % N_MIXERS, i // N_MIXERS
        x = x + 0.5 * swiglu(rmsnorm(x, norm_gains[i, 0]), ffn_w_in[i, 0], ffn_w_out[i, 0])
        h = rmsnorm(x, norm_gains[i, 1])
        if m == 0:
            y = dilated_window_attention(h, positions, a_w_in[j], a_q_gain[j], a_k_gain[j], a_w_out[j])
        elif m == 1:
            y = latent_attention(h, positions, b_w_in[j], b_q_a_gain[j], b_w_q_up[j], b_kv_a_gain[j],
                                 b_w_kv_up[j], b_q_gain[j], b_k_gain[j], b_w_out[j])
        elif m == 2:
            y = retention(h, positions, c_w_in[j], c_w_out[j])
        else:
            y = recurrent_block(h, d_w_in[j], d_conv_w[j], d_conv_b[j], d_w_rg[j], d_b_rg[j],
                                d_w_ig[j], d_b_ig[j], d_lru_param[j], d_w_out[j])
        x = x + y
        x = x + 0.5 * swiglu(rmsnorm(x, norm_gains[i, 2]), ffn_w_in[i, 1], ffn_w_out[i, 1])
    return x
```

```python
import functools

import jax
import jax.numpy as jnp
from jax import lax
from jax.experimental import pallas as pl
from jax.experimental.pallas import tpu as pltpu

F32 = jnp.float32
BF16 = jnp.bfloat16

D_MODEL = 1024
D_FF = 2816
NORM_EPS = 1e-6
GN_EPS = 1e-5
ROPE_THETA = 10000.0
NEG_INF = -1e30

A_HEADS = 16
A_HEAD_DIM = 64
A_PATTERNS = ((128, 1), (512, 4), (2048, 16))
A_WINDOW_STEPS = 128
A_GROUP_WIDTH = A_HEADS * A_HEAD_DIM

B_HEADS = 16
B_NOPE = 64
B_ROPE = 32
B_QK = B_NOPE + B_ROPE
B_VDIM = 64
B_Q_RANK = 384
B_KV_RANK = 256

C_HEADS = 8
C_KDIM = 128
C_VDIM = 256
C_CHUNK = 256

D_WIDTH = 1024
D_BLOCKS = 4
D_BLOCK = D_WIDTH // D_BLOCKS
D_CONV = 4
LRU_C = 8.0

LANES = 128
SUBLANES = 8
VMEM_LIMIT_BYTES = 52 * 1024 * 1024


def _params(*semantics):
    return pltpu.CompilerParams(dimension_semantics=semantics, vmem_limit_bytes=VMEM_LIMIT_BYTES)


def _rms(x, g):
    ms = jnp.mean(x * x, axis=-1, keepdims=True)
    return x * lax.rsqrt(ms + NORM_EPS) * g


def _expm1(x):
    u = jnp.exp(x)
    near = u == 1.0
    ratio = (u - 1.0) * x / jnp.where(near, 1.0, jnp.log(u))
    return jnp.where(x < -1.0, u - 1.0, jnp.where(near, x, ratio))


def _const_spec(shape):
    nd = len(shape)
    return pl.BlockSpec(shape, lambda *_: (0,) * nd, pipeline_mode=pl.Buffered(1))


def _trig_body(pos_ref, inv_ref, cos_ref, sin_ref):
    ang = pos_ref[...] * inv_ref[...]
    cos_ref[...] = jnp.cos(ang)
    sin_ref[...] = jnp.sin(ang)


def _rope_trig(positions, half):
    t = positions.size
    per_row = LANES // half
    rows = t // per_row
    pos_rep = jnp.repeat(positions.reshape(rows, per_row).astype(F32), half, axis=1)
    inv = ROPE_THETA ** (-jnp.arange(half, dtype=F32) * 2.0 / (2 * half))
    inv_row = jnp.tile(inv, per_row)[None, :]
    tr = min(rows, 2048)
    cos, sin = pl.pallas_call(
        _trig_body,
        grid=(rows // tr,),
        in_specs=[pl.BlockSpec((tr, LANES), lambda i: (i, 0)), pl.BlockSpec((1, LANES), lambda i: (0, 0))],
        out_specs=[pl.BlockSpec((tr, LANES), lambda i: (i, 0))] * 2,
        out_shape=[jax.ShapeDtypeStruct((rows, LANES), F32)] * 2,
        compiler_params=_params("parallel"),
        name="rope_trig",
    )(pos_rep, inv_row)
    return cos.reshape(t, half), sin.reshape(t, half)


FFN_TM = 512
FFN_TF = 256


def _ffn_body(x_ref, g_ref, wg_ref, wu_ref, wo_ref, o_ref, act_ref):
    x = x_ref[...]
    xn = _rms(x, g_ref[...]).astype(BF16)
    for j in range(D_FF // FFN_TF):
        sl = slice(j * FFN_TF, (j + 1) * FFN_TF)
        gate = jnp.dot(xn, wg_ref[:, sl], preferred_element_type=F32)
        up = jnp.dot(xn, wu_ref[:, sl], preferred_element_type=F32)
        act_ref[:, sl] = (gate * jax.nn.sigmoid(gate) * up).astype(BF16)
    y = jnp.dot(act_ref[...], wo_ref[...], preferred_element_type=F32)
    o_ref[...] = x + 0.5 * y


def _ffn(x, gain, w_in, w_out):
    t = x.shape[0]
    tm = min(FFN_TM, t)
    wg = w_in[:, :D_FF].astype(BF16)
    wu = w_in[:, D_FF:].astype(BF16)
    wo = w_out.astype(BF16)
    return pl.pallas_call(
        _ffn_body,
        grid=(t // tm,),
        in_specs=[
            pl.BlockSpec((tm, D_MODEL), lambda i: (i, 0)),
            _const_spec((1, D_MODEL)),
            _const_spec((D_MODEL, D_FF)),
            _const_spec((D_MODEL, D_FF)),
            _const_spec((D_FF, D_MODEL)),
        ],
        out_specs=pl.BlockSpec((tm, D_MODEL), lambda i: (i, 0)),
        out_shape=jax.ShapeDtypeStruct((t, D_MODEL), F32),
        scratch_shapes=[pltpu.VMEM((tm, D_FF), BF16)],
        compiler_params=_params("parallel"),
        name="ffn",
    )(x, gain[None, :], wg, wu, wo)


PROJ_TM = 512


def _proj_res_body(x_ref, a_ref, w_ref, o_ref):
    o_ref[...] = x_ref[...] + jnp.dot(a_ref[...], w_ref[...], preferred_element_type=F32)


def _proj_res_gated_body(x_ref, a_ref, b_ref, w_ref, o_ref):
    a = (a_ref[...].astype(F32) * b_ref[...].astype(F32)).astype(BF16)
    o_ref[...] = x_ref[...] + jnp.dot(a, w_ref[...], preferred_element_type=F32)


def _proj_res(x, a, w, gate=None):
    t, k = a.shape
    tm = min(PROJ_TM, t)
    row = lambda i: (i, 0)
    acts = [a] if gate is None else [a, gate]
    body = _proj_res_body if gate is None else _proj_res_gated_body
    return pl.pallas_call(
        body,
        grid=(t // tm,),
        in_specs=[pl.BlockSpec((tm, D_MODEL), row)] + [pl.BlockSpec((tm, k), row)] * len(acts)
        + [_const_spec((k, D_MODEL))],
        out_specs=pl.BlockSpec((tm, D_MODEL), row),
        out_shape=jax.ShapeDtypeStruct((t, D_MODEL), F32),
        compiler_params=_params("parallel"),
        name="proj_res",
    )(x, *acts, w.astype(BF16))


A_TM = 512
A_TQ = 512


def _a_proj_body(x_ref, g_ref, w_ref, gain_ref, c_ref, sa_ref, sb_ref, o_ref, xn_ref):
    j = pl.program_id(1)
    width = w_ref.shape[1]

    @pl.when(j == 0)
    def _():
        xn_ref[...] = _rms(x_ref[...], g_ref[...]).astype(BF16)

    xn = xn_ref[...]

    @pl.when(j < 2)
    def _():
        gain = gain_ref[0]
        cos, sin_a, sin_b = c_ref[...], sa_ref[...], sb_ref[...]
        first = lax.broadcasted_iota(jnp.int32, cos.shape, 1) < A_HEAD_DIM
        for c in range(width // 256):
            y2 = jnp.dot(xn, w_ref[:, c * 256:(c + 1) * 256], preferred_element_type=F32)
            for hh in range(2):
                y = y2[:, hh * LANES:(hh + 1) * LANES]
                sq = y * y
                s0 = jnp.sum(jnp.where(first, sq, 0.0), axis=-1, keepdims=True)
                s1 = jnp.sum(jnp.where(first, 0.0, sq), axis=-1, keepdims=True)
                ms = jnp.where(first, s0, s1) * (1.0 / A_HEAD_DIM)
                yn = y * lax.rsqrt(ms + NORM_EPS) * gain
                out = yn * cos + pltpu.roll(yn, 96, 1) * sin_a + pltpu.roll(yn, 32, 1) * sin_b
                lo = c * 256 + hh * LANES
                o_ref[0, :, lo:lo + LANES] = out.astype(BF16)

    @pl.when(j == 2)
    def _():
        for c in range(width // 256):
            sl = slice(c * 256, (c + 1) * 256)
            o_ref[0, :, sl] = jnp.dot(xn, w_ref[:, sl], preferred_element_type=F32).astype(BF16)


def _a_proj(x, gain, w_in, q_gain, k_gain, cos, sin_a, sin_b):
    t = x.shape[0]
    tm = min(A_TM, t)
    width = 3 * A_GROUP_WIDTH
    gains = jnp.stack([jnp.tile(q_gain, 2), jnp.tile(k_gain, 2)])[:, None, :]
    tab = pl.BlockSpec((tm, LANES), lambda i, j: (i, 0))
    return pl.pallas_call(
        _a_proj_body,
        grid=(t // tm, 3),
        in_specs=[
            pl.BlockSpec((tm, D_MODEL), lambda i, j: (i, 0)),
            pl.BlockSpec((1, D_MODEL), lambda i, j: (0, 0)),
            pl.BlockSpec((D_MODEL, width), lambda i, j: (0, j)),
            pl.BlockSpec((1, 1, LANES), lambda i, j: (jnp.minimum(j, 1), 0, 0)),
            tab, tab, tab,
        ],
        out_specs=pl.BlockSpec((1, tm, width), lambda i, j: (j, i, 0)),
        out_shape=jax.ShapeDtypeStruct((3, t, width), BF16),
        scratch_shapes=[pltpu.VMEM((tm, D_MODEL), BF16)],
        compiler_params=_params("parallel", "arbitrary"),
        name="a_proj",
    )(x, gain[None, :], w_in.astype(BF16), gains, cos, sin_a, sin_b)


def _a_attn_body(q_ref, kc_ref, kp_ref, vc_ref, vp_ref, o_ref, lse_ref, *, blocks_per_seq):
    g = pl.program_id(0)
    i = pl.program_id(1)
    hp = pl.program_id(2)
    w = A_WINDOW_STEPS
    nblk = q_ref.shape[2] // w
    nb = jnp.where(g == 0, blocks_per_seq[0], jnp.where(g == 1, blocks_per_seq[1], blocks_per_seq[2]))

    lane = lax.broadcasted_iota(jnp.int32, (w, LANES), 1)
    even = lane < A_HEAD_DIM
    rows = lax.broadcasted_iota(jnp.int32, (2 * w, 2 * w), 0)
    u = jnp.where(rows >= w, rows - w, rows)
    c = lax.broadcasted_iota(jnp.int32, (2 * w, 2 * w), 1)
    band = (c >= u) & (c <= u + w)
    lane16 = lax.broadcasted_iota(jnp.int32, (w, A_HEADS), 1)
    scale = A_HEAD_DIM ** -0.5

    @pl.when(hp == 0)
    def _():
        lse_ref[...] = jnp.zeros_like(lse_ref)

    for j in range(nblk):
        rs = slice(j * w, (j + 1) * w)
        qb = q_ref[0, 0, rs, :]
        zero = jnp.zeros_like(qb)
        q2 = jnp.concatenate([jnp.where(even, qb, zero), jnp.where(even, zero, qb)], axis=0)
        if j == 0:
            kk = jnp.concatenate([kp_ref[0, 0], kc_ref[0, 0, rs, :]], axis=0)
            vv = jnp.concatenate([vp_ref[0, 0], vc_ref[0, 0, rs, :]], axis=0)
        else:
            kk = kc_ref[0, 0, (j - 1) * w:(j + 1) * w, :]
            vv = vc_ref[0, 0, (j - 1) * w:(j + 1) * w, :]
        min_col = jnp.where(((i * nblk + j) & (nb - 1)) == 0, w, 0)
        s = lax.dot_general(q2, kk, (((1,), (1,)), ((), ())), preferred_element_type=F32) * scale
        valid = band & (c >= min_col)
        s = jnp.where(valid, s, NEG_INF)
        m = jnp.max(s, axis=-1, keepdims=True)
        p = jnp.exp(s - m)
        l = jnp.sum(p, axis=-1, keepdims=True)
        lse = m + jnp.log(l)
        pv = jnp.dot(p.astype(BF16), vv, preferred_element_type=F32)
        inv = 1.0 / l
        o = jnp.where(even, pv[:w] * inv[:w], pv[w:] * inv[w:])
        o_ref[0, rs, :] = o.astype(BF16)
        old = lse_ref[0, rs, :]
        lse_ref[0, rs, :] = jnp.where(lane16 == 2 * hp, lse[:w], jnp.where(lane16 == 2 * hp + 1, lse[w:], old))


def _a_attn(qkv, seq_len):
    t = qkv.shape[2]
    tq = min(A_TQ, t)
    w = A_WINDOW_STEPS
    nblk = tq // w
    blocks_per_seq = tuple(seq_len // d // w for _, d in A_PATTERNS)
    cur = lambda which: pl.BlockSpec((1, 1, tq, LANES), lambda g, i, hp: (g, which, i, hp))
    prev = lambda which: pl.BlockSpec(
        (1, 1, w, LANES), lambda g, i, hp: (g, which, jnp.maximum(i * nblk - 1, 0), hp))
    return pl.pallas_call(
        functools.partial(_a_attn_body, blocks_per_seq=blocks_per_seq),
        grid=(3, t // tq, A_HEADS // 2),
        in_specs=[cur(0), cur(1), prev(1), cur(2), prev(2)],
        out_specs=[
            pl.BlockSpec((1, tq, LANES), lambda g, i, hp: (g, i, hp)),
            pl.BlockSpec((1, tq, A_HEADS), lambda g, i, hp: (g, i, 0)),
        ],
        out_shape=[
            jax.ShapeDtypeStruct((3, t, A_GROUP_WIDTH), BF16),
            jax.ShapeDtypeStruct((3, t, A_HEADS), F32),
        ],
        compiler_params=_params("parallel", "parallel", "arbitrary"),
        name="a_attn",
    )(qkv, qkv, qkv, qkv, qkv)


def _a_out_body(x_ref, o_ref, lse_ref, e_ref, w_ref, out_ref):
    l0, l1, l2 = lse_ref[0], lse_ref[1], lse_ref[2]
    m = jnp.maximum(jnp.maximum(l0, l1), l2)
    e0, e1, e2 = jnp.exp(l0 - m), jnp.exp(l1 - m), jnp.exp(l2 - m)
    inv = 1.0 / (e0 + e1 + e2)
    merged = None
    for gi, e in enumerate((e0, e1, e2)):
        wt = e * inv
        hi = wt.astype(BF16)
        lo = (wt - hi.astype(F32)).astype(BF16)
        wexp = (jnp.dot(hi, e_ref[...], preferred_element_type=F32)
                + jnp.dot(lo, e_ref[...], preferred_element_type=F32))
        term = wexp * o_ref[gi].astype(F32)
        merged = term if merged is None else merged + term
    out_ref[...] = x_ref[...] + jnp.dot(merged.astype(BF16), w_ref[...], preferred_element_type=F32)


def _a_out(x, o3, lse3, w_out):
    t = x.shape[0]
    tm = min(PROJ_TM, t)
    expand = jnp.repeat(jnp.eye(A_HEADS, dtype=BF16), A_HEAD_DIM, axis=1)
    return pl.pallas_call(
        _a_out_body,
        grid=(t // tm,),
        in_specs=[
            pl.BlockSpec((tm, D_MODEL), lambda i: (i, 0)),
            pl.BlockSpec((3, tm, A_GROUP_WIDTH), lambda i: (0, i, 0)),
            pl.BlockSpec((3, tm, A_HEADS), lambda i: (0, i, 0)),
            _const_spec((A_HEADS, A_GROUP_WIDTH)),
            _const_spec((A_GROUP_WIDTH, D_MODEL)),
        ],
        out_specs=pl.BlockSpec((tm, D_MODEL), lambda i: (i, 0)),
        out_shape=jax.ShapeDtypeStruct((t, D_MODEL), F32),
        compiler_params=_params("parallel"),
        name="a_out",
    )(x, o3, lse3, expand, w_out.astype(BF16))


def _mixer_a(x, gain, batch, seq_len, trig, w_in, q_gain, k_gain, w_out):
    t = x.shape[0]
    cos, sin = trig
    zero = jnp.zeros_like(sin)
    tab_c = jnp.tile(jnp.concatenate([cos, cos], axis=1), (1, 2))
    tab_sa = jnp.tile(jnp.concatenate([-sin, zero], axis=1), (1, 2))
    tab_sb = jnp.tile(jnp.concatenate([zero, sin], axis=1), (1, 2))
    qkv = _a_proj(x, gain, w_in, q_gain, k_gain, tab_c, tab_sa, tab_sb)
    groups = []
    for gi, (_, d) in enumerate(A_PATTERNS):
        part = qkv[:, :, gi * A_GROUP_WIDTH:(gi + 1) * A_GROUP_WIDTH]
        part = part.reshape(3, batch, seq_len // d, d, A_GROUP_WIDTH).transpose(0, 1, 3, 2, 4)
        groups.append(part.reshape(3, t, A_GROUP_WIDTH))
    o3, lse3 = _a_attn(jnp.stack(groups), seq_len)
    outs, lses = [], []
    for gi, (_, d) in enumerate(A_PATTERNS):
        o = o3[gi].reshape(batch, d, seq_len // d, A_GROUP_WIDTH).transpose(0, 2, 1, 3)
        outs.append(o.reshape(t, A_GROUP_WIDTH))
        l = lse3[gi].reshape(batch, d, seq_len // d, A_HEADS).transpose(0, 2, 1, 3)
        lses.append(l.reshape(t, A_HEADS))
    return _a_out(x, jnp.stack(outs), jnp.stack(lses), w_out)


B_TM = 512
B_TQ = 512
B_HEAD_PAD = LANES
B_IN_PAD = 768


def _b_proj_body(x_ref, g_ref, win_ref, qag_ref, kvag_ref, wq_ref, wkv_ref, qg_ref, kg_ref,
                 c_ref, sa_ref, sb_ref, q_ref, k_ref, v_ref):
    xn = _rms(x_ref[...], g_ref[...]).astype(BF16)
    h = jnp.dot(xn, win_ref[...], preferred_element_type=F32)
    c_q = h[:, :B_Q_RANK]
    rest = h[:, B_Q_RANK:]
    cq = (_rms(c_q, qag_ref[...])).astype(BF16)
    lane_r = lax.broadcasted_iota(jnp.int32, rest.shape, 1)
    is_kv = lane_r < B_KV_RANK
    ms = jnp.sum(jnp.where(is_kv, rest * rest, 0.0), axis=-1, keepdims=True) * (1.0 / B_KV_RANK)
    ckv = jnp.where(is_kv, rest * lax.rsqrt(ms + NORM_EPS) * kvag_ref[...], rest).astype(BF16)
    cos, sin_a, sin_b = c_ref[...], sa_ref[...], sb_ref[...]

    def head_norm_rope(y, gain):
        ms_h = jnp.sum(y * y, axis=-1, keepdims=True) * (1.0 / B_QK)
        yn = y * lax.rsqrt(ms_h + NORM_EPS) * gain
        return yn * cos + pltpu.roll(yn, LANES - B_ROPE // 2, 1) * sin_a + pltpu.roll(yn, B_ROPE // 2, 1) * sin_b

    for hd in range(B_HEADS // 2):
        sl = slice(hd * 256, (hd + 1) * 256)
        q2 = jnp.dot(cq, wq_ref[:, sl], preferred_element_type=F32)
        k2 = jnp.dot(ckv, wkv_ref[:, sl], preferred_element_type=F32)
        for hh in range(2):
            lo = hd * 256 + hh * LANES
            q_ref[:, lo:lo + LANES] = head_norm_rope(q2[:, hh * LANES:(hh + 1) * LANES], qg_ref[...]).astype(BF16)
            k_ref[:, lo:lo + LANES] = head_norm_rope(k2[:, hh * LANES:(hh + 1) * LANES], kg_ref[...]).astype(BF16)
    k_width = B_HEADS * B_HEAD_PAD
    for c in range(B_HEADS * B_VDIM // 256):
        sl = slice(k_width + c * 256, k_width + (c + 1) * 256)
        v_ref[:, c * 256:(c + 1) * 256] = jnp.dot(ckv, wkv_ref[:, sl], preferred_element_type=F32).astype(BF16)


def _b_proj(x, gain, w_in, q_a_gain, w_q_up, kv_a_gain, w_kv_up, q_gain, k_gain, cos, sin_a, sin_b):
    t = x.shape[0]
    tm = min(B_TM, t)
    in_w = B_Q_RANK + B_KV_RANK + B_ROPE
    win = jnp.pad(w_in, ((0, 0), (0, B_IN_PAD - in_w))).astype(BF16)
    wq = jnp.pad(w_q_up.reshape(B_Q_RANK, B_HEADS, B_QK), ((0, 0), (0, 0), (0, B_HEAD_PAD - B_QK)))
    wq = wq.reshape(B_Q_RANK, B_HEADS * B_HEAD_PAD).astype(BF16)
    kv_in = B_IN_PAD - B_Q_RANK
    wkv = w_kv_up.reshape(B_KV_RANK, B_HEADS, B_NOPE + B_VDIM)
    wk = jnp.zeros((kv_in, B_HEADS, B_HEAD_PAD), F32)
    wk = wk.at[:B_KV_RANK, :, :B_NOPE].set(wkv[:, :, :B_NOPE])
    wk = wk.at[B_KV_RANK:B_KV_RANK + B_ROPE, :, B_NOPE:B_QK].set(
        jnp.broadcast_to(jnp.eye(B_ROPE, dtype=F32)[:, None, :], (B_ROPE, B_HEADS, B_ROPE)))
    wv = jnp.zeros((kv_in, B_HEADS, B_VDIM), F32).at[:B_KV_RANK].set(wkv[:, :, B_NOPE:])
    wkv_full = jnp.concatenate(
        [wk.reshape(kv_in, B_HEADS * B_HEAD_PAD), wv.reshape(kv_in, B_HEADS * B_VDIM)], axis=1).astype(BF16)
    pad_gain = lambda gn: jnp.pad(gn, (0, B_HEAD_PAD - B_QK))[None, :]
    kvag = jnp.pad(kv_a_gain, (0, kv_in - B_KV_RANK))[None, :]
    row = lambda i: (i, 0)
    tab = pl.BlockSpec((tm, LANES), row)
    return pl.pallas_call(
        _b_proj_body,
        grid=(t // tm,),
        in_specs=[
            pl.BlockSpec((tm, D_MODEL), row),
            _const_spec((1, D_MODEL)),
            _const_spec((D_MODEL, B_IN_PAD)),
            _const_spec((1, B_Q_RANK)),
            _const_spec((1, kv_in)),
            _const_spec((B_Q_RANK, B_HEADS * B_HEAD_PAD)),
            _const_spec((kv_in, B_HEADS * (B_HEAD_PAD + B_VDIM))),
            _const_spec((1, B_HEAD_PAD)),
            _const_spec((1, B_HEAD_PAD)),
            tab, tab, tab,
        ],
        out_specs=[
            pl.BlockSpec((tm, B_HEADS * B_HEAD_PAD), row),
            pl.BlockSpec((tm, B_HEADS * B_HEAD_PAD), row),
            pl.BlockSpec((tm, B_HEADS * B_VDIM), row),
        ],
        out_shape=[
            jax.ShapeDtypeStruct((t, B_HEADS * B_HEAD_PAD), BF16),
            jax.ShapeDtypeStruct((t, B_HEADS * B_HEAD_PAD), BF16),
            jax.ShapeDtypeStruct((t, B_HEADS * B_VDIM), BF16),
        ],
        compiler_params=_params("parallel"),
        name="b_proj",
    )(x, gain[None, :], win, q_a_gain[None, :], kvag, wq, wkv_full, pad_gain(q_gain), pad_gain(k_gain),
      cos, sin_a, sin_b)


def _b_attn_body(q_ref, k_ref, v_ref, o_ref, m_ref, l_ref, acc_ref):
    qi = pl.program_id(2)
    tq = q_ref.shape[1]
    scale = B_QK ** -0.5
    m_ref[...] = jnp.full(m_ref.shape, NEG_INF, F32)
    l_ref[...] = jnp.zeros(l_ref.shape, F32)
    acc_ref[...] = jnp.zeros(acc_ref.shape, F32)
    qpos = qi * tq + lax.broadcasted_iota(jnp.int32, (tq, tq), 0)
    kidx = lax.broadcasted_iota(jnp.int32, (tq, tq), 1)
    lane = lax.broadcasted_iota(jnp.int32, (tq, LANES), 1)
    even = lane < B_VDIM

    def step(kj, carry):
        ks = pl.ds(pl.multiple_of(kj * tq, tq), tq)
        vv = v_ref[0, ks, :]
        causal = (kj * tq + kidx) <= qpos
        for hh in range(2):
            q = q_ref[0, :, hh * LANES:(hh + 1) * LANES]
            k = k_ref[0, ks, hh * LANES:(hh + 1) * LANES]
            s = lax.dot_general(q, k, (((1,), (1,)), ((), ())), preferred_element_type=F32) * scale
            s = jnp.where(causal, s, NEG_INF)
            m_old = m_ref[hh]
            m_new = jnp.maximum(m_old, jnp.max(s, axis=-1, keepdims=True))
            alpha = jnp.exp(m_old - m_new)
            p = jnp.exp(s - m_new)
            l_ref[hh] = alpha * l_ref[hh] + jnp.sum(p, axis=-1, keepdims=True)
            m_ref[hh] = m_new
            pv = jnp.dot(p.astype(BF16), vv, preferred_element_type=F32)
            acc_ref[hh] = alpha * acc_ref[hh] + pv
        return carry

    lax.fori_loop(0, qi + 1, step, 0)
    o0 = acc_ref[0] * (1.0 / l_ref[0])
    o1 = acc_ref[1] * (1.0 / l_ref[1])
    o_ref[0] = jnp.where(even, o0, o1).astype(BF16)


def _b_attn(q, k, v, batch, seq_len):
    tq = min(B_TQ, seq_len)
    return pl.pallas_call(
        _b_attn_body,
        grid=(batch, B_HEADS // 2, seq_len // tq),
        in_specs=[
            pl.BlockSpec((1, tq, 2 * B_HEAD_PAD), lambda b, hp, qi: (b, qi, hp)),
            pl.BlockSpec((1, seq_len, 2 * B_HEAD_PAD), lambda b, hp, qi: (b, 0, hp)),
            pl.BlockSpec((1, seq_len, 2 * B_VDIM), lambda b, hp, qi: (b, 0, hp)),
        ],
        out_specs=pl.BlockSpec((1, tq, 2 * B_VDIM), lambda b, hp, qi: (b, qi, hp)),
        out_shape=jax.ShapeDtypeStruct((batch, seq_len, B_HEADS * B_VDIM), BF16),
        scratch_shapes=[
            pltpu.VMEM((2, tq, 1), F32),
            pltpu.VMEM((2, tq, 1), F32),
            pltpu.VMEM((2, tq, 2 * B_VDIM), F32),
        ],
        compiler_params=_params("parallel", "parallel", "arbitrary"),
        name="b_attn",
    )(q, k, v)


def _mixer_b(x, gain, batch, seq_len, trig, w_in, q_a_gain, w_q_up, kv_a_gain, w_kv_up, q_gain, k_gain, w_out):
    t = x.shape[0]
    cos, sin = trig
    one = jnp.ones((t, B_NOPE), F32)
    zero64 = jnp.zeros((t, B_NOPE), F32)
    z16 = jnp.zeros_like(sin)
    tail = jnp.zeros((t, B_HEAD_PAD - B_QK), F32)
    tab_c = jnp.concatenate([one, cos, cos, tail + 1.0], axis=1)
    tab_sa = jnp.concatenate([zero64, -sin, z16, tail], axis=1)
    tab_sb = jnp.concatenate([zero64, z16, sin, tail], axis=1)
    q, k, v = _b_proj(x, gain, w_in, q_a_gain, w_q_up, kv_a_gain, w_kv_up, q_gain, k_gain, tab_c, tab_sa, tab_sb)
    shp = lambda a: a.reshape(batch, seq_len, a.shape[-1])
    o = _b_attn(shp(q), shp(k), shp(v), batch, seq_len)
    return _proj_res(x, o.reshape(t, B_HEADS * B_VDIM), w_out)


C_TM = 512
C_QK_W = C_HEADS * C_KDIM
C_V_W = C_HEADS * C_VDIM


def _c_proj_body(x_ref, g_ref, w_ref, c_ref, s_ref, o_ref, xn_ref):
    j = pl.program_id(1)
    width = w_ref.shape[1]

    @pl.when(j == 0)
    def _():
        xn_ref[...] = _rms(x_ref[...], g_ref[...]).astype(BF16)

    xn = xn_ref[...]

    @pl.when(j == 0)
    def _():
        cos, sin = c_ref[...], s_ref[...]
        for c in range(width // 256):
            y2 = jnp.dot(xn, w_ref[:, c * 256:(c + 1) * 256], preferred_element_type=F32)
            for hh in range(2):
                y = y2[:, hh * LANES:(hh + 1) * LANES]
                out = y * cos + pltpu.roll(y, C_KDIM // 2, 1) * sin
                lo = c * 256 + hh * LANES
                if lo >= C_QK_W:
                    out = out * (C_KDIM ** -0.5)
                o_ref[0, :, lo:lo + LANES] = out.astype(BF16)

    @pl.when(j == 1)
    def _():
        for c in range(width // 256):
            sl = slice(c * 256, (c + 1) * 256)
            o_ref[0, :, sl] = jnp.dot(xn, w_ref[:, sl], preferred_element_type=F32).astype(BF16)

    @pl.when(j == 2)
    def _():
        for c in range(width // 256):
            sl = slice(c * 256, (c + 1) * 256)
            gt = jnp.dot(xn, w_ref[:, sl], preferred_element_type=F32)
            o_ref[0, :, sl] = (gt * jax.nn.sigmoid(gt)).astype(BF16)


def _c_proj(x, gain, w_in, cos, sin):
    t = x.shape[0]
    tm = min(C_TM, t)
    width = 2 * C_QK_W
    tab = pl.BlockSpec((tm, LANES), lambda i, j: (i, 0))
    return pl.pallas_call(
        _c_proj_body,
        grid=(t // tm, 3),
        in_specs=[
            pl.BlockSpec((tm, D_MODEL), lambda i, j: (i, 0)),
            pl.BlockSpec((1, D_MODEL), lambda i, j: (0, 0)),
            pl.BlockSpec((D_MODEL, width), lambda i, j: (0, j)),
            tab, tab,
        ],
        out_specs=pl.BlockSpec((1, tm, width), lambda i, j: (j, i, 0)),
        out_shape=jax.ShapeDtypeStruct((3, t, width), BF16),
        scratch_shapes=[pltpu.VMEM((tm, D_MODEL), BF16)],
        compiler_params=_params("parallel", "arbitrary"),
        name="c_proj",
    )(x, gain[None, :], w_in.astype(BF16), cos, sin)


def _c_ret_body(q_ref, k_ref, v_ref, dec_ref, xi_ref, zeta_ref, cd_ref, y_ref, r_ref):
    n_chunks = q_ref.shape[2] // C_CHUNK
    r_ref[...] = jnp.zeros_like(r_ref)
    decay = dec_ref[0]
    xi = xi_ref[0]
    zeta = zeta_ref[0]
    cd = cd_ref[0]

    def step(n, carry):
        rows = pl.ds(pl.multiple_of(n * C_CHUNK, C_CHUNK), C_CHUNK)
        qc = q_ref[0, 0, rows, :]
        kc = k_ref[0, 0, rows, :]
        vc = v_ref[0, 0, rows, :]
        s = lax.dot_general(qc, kc, (((1,), (1,)), ((), ())), preferred_element_type=F32) * decay
        inner = jnp.dot(s.astype(BF16), vc, preferred_element_type=F32)
        r_old = r_ref[...]
        cross = jnp.dot(qc, r_old.astype(BF16), preferred_element_type=F32) * xi
        kz = (kc.astype(F32) * zeta).astype(BF16)
        r_ref[...] = cd * r_old + lax.dot_general(kz, vc, (((0,), (0,)), ((), ())), preferred_element_type=F32)
        y = inner + cross
        mu = jnp.mean(y, axis=-1, keepdims=True)
        yc = y - mu
        var = jnp.mean(yc * yc, axis=-1, keepdims=True)
        y_ref[0, rows, :] = (yc * lax.rsqrt(var + GN_EPS)).astype(BF16)
        return carry

    lax.fori_loop(0, n_chunks, step, 0)


def _c_ret(qkv, batch, seq_len):
    cc = C_CHUNK
    log_g = jnp.log(1.0 - 2.0 ** (-5.0 - jnp.arange(C_HEADS, dtype=F32)))
    idx = jnp.arange(cc, dtype=F32)
    diff = idx[:, None] - idx[None, :]
    causal = diff >= 0
    decay = jnp.where(causal[None], jnp.exp(jnp.where(causal, diff, 0.0)[None] * log_g[:, None, None]), 0.0)
    xi = jnp.exp((idx + 1.0)[None, :] * log_g[:, None])[:, :, None]
    zeta = jnp.exp((cc - 1.0 - idx)[None, :] * log_g[:, None])[:, :, None]
    cdec = jnp.broadcast_to(jnp.exp(cc * log_g)[:, None, None], (C_HEADS, 1, C_VDIM))
    per_head = lambda shape: pl.BlockSpec((1,) + shape, lambda b, h: (h, 0, 0))
    return pl.pallas_call(
        _c_ret_body,
        grid=(batch, C_HEADS),
        in_specs=[
            pl.BlockSpec((1, 1, seq_len, C_KDIM), lambda b, h: (0, b, 0, h)),
            pl.BlockSpec((1, 1, seq_len, C_KDIM), lambda b, h: (0, b, 0, C_HEADS + h)),
            pl.BlockSpec((1, 1, seq_len, C_VDIM), lambda b, h: (1, b, 0, h)),
            per_head((cc, cc)), per_head((cc, 1)), per_head((cc, 1)), per_head((1, C_VDIM)),
        ],
        out_specs=pl.BlockSpec((1, seq_len, C_VDIM), lambda b, h: (b, 0, h)),
        out_shape=jax.ShapeDtypeStruct((batch, seq_len, C_V_W), BF16),
        scratch_shapes=[pltpu.VMEM((C_KDIM, C_VDIM), F32)],
        compiler_params=_params("parallel", "parallel"),
        name="c_ret",
    )(qkv, qkv, qkv, decay, xi, zeta, cdec)


def _mixer_c(x, gain, batch, seq_len, trig, w_in, w_out):
    t = x.shape[0]
    cos, sin = trig
    tab_c = jnp.concatenate([cos, cos], axis=1)
    tab_s = jnp.concatenate([-sin, sin], axis=1)
    proj = _c_proj(x, gain, w_in, tab_c, tab_s)
    y = _c_ret(proj.reshape(3, batch, seq_len, 2 * C_QK_W), batch, seq_len)
    return _proj_res(x, y.reshape(t, C_V_W), w_out, gate=proj[2])


D_TS = 256


def _d_body(x_ref, g_ref, win_ref, cw_ref, cb_ref, wrg_ref, brg_ref, wig_ref, big_ref, lru_ref,
            o_ref, ubuf_ref, a_ref, b_ref, hs_ref, h_ref):
    ti = pl.program_id(1)
    ts = x_ref.shape[1]
    pad = SUBLANES

    @pl.when(ti == 0)
    def _():
        ubuf_ref[0:pad, :] = jnp.zeros((pad, D_WIDTH), F32)
        h_ref[...] = jnp.zeros_like(h_ref)

    xn = _rms(x_ref[0], g_ref[...]).astype(BF16)
    gate = jnp.dot(xn, win_ref[:, :D_WIDTH], preferred_element_type=F32)
    u = jnp.dot(xn, win_ref[:, D_WIDTH:], preferred_element_type=F32)
    ubuf_ref[pad:pad + ts, :] = u
    uc = cb_ref[...] + cw_ref[D_CONV - 1:D_CONV, :] * u
    for k in range(D_CONV - 1):
        off = pad - (D_CONV - 1) + k
        uc = uc + cw_ref[k:k + 1, :] * ubuf_ref[off:off + ts, :]
    ubuf_ref[0:pad, :] = ubuf_ref[ts:ts + pad, :]

    ucb = uc.astype(BF16)
    softplus_neg = jax.nn.softplus(-lru_ref[...])
    for n in range(D_BLOCKS):
        sl = slice(n * D_BLOCK, (n + 1) * D_BLOCK)
        r = jax.nn.sigmoid(jnp.dot(ucb[:, sl], wrg_ref[n], preferred_element_type=F32) + brg_ref[:, sl])
        ig = jax.nn.sigmoid(jnp.dot(ucb[:, sl], wig_ref[n], preferred_element_type=F32) + big_ref[:, sl])
        log_a = -LRU_C * r * softplus_neg[:, sl]
        a_ref[:, sl] = jnp.exp(log_a)
        b_ref[:, sl] = jnp.sqrt(jnp.maximum(-_expm1(2.0 * log_a), 0.0)) * (ig * uc[:, sl])

    row = lax.broadcasted_iota(jnp.int32, (SUBLANES, D_WIDTH), 0)

    def scan_group(j, h):
        rows = pl.ds(pl.multiple_of(j * SUBLANES, SUBLANES), SUBLANES)
        a = a_ref[rows, :]
        b = b_ref[rows, :]
        for s in (1, 2, 4):
            keep = row >= s
            b = jnp.where(keep, a * pltpu.roll(b, s, 0) + b, b)
            a = jnp.where(keep, a * pltpu.roll(a, s, 0), a)
        hs = a * h + b
        hs_ref[rows, :] = hs
        return jnp.broadcast_to(hs[SUBLANES - 1:SUBLANES, :], (SUBLANES, D_WIDTH))

    h_ref[...] = lax.fori_loop(0, ts // SUBLANES, scan_group, h_ref[...])
    o_ref[0] = (jax.nn.gelu(gate) * hs_ref[...]).astype(BF16)


def _d_main(x3, gain, w_in, conv_w, conv_b, w_rg, b_rg, w_ig, b_ig, lru_param):
    batch, seq_len, _ = x3.shape
    ts = min(D_TS, seq_len)
    vec = lambda a: a[None, :]
    return pl.pallas_call(
        _d_body,
        grid=(batch, seq_len // ts),
        in_specs=[
            pl.BlockSpec((1, ts, D_MODEL), lambda b, i: (b, i, 0)),
            _const_spec((1, D_MODEL)),
            _const_spec((D_MODEL, 2 * D_WIDTH)),
            _const_spec((D_CONV, D_WIDTH)),
            _const_spec((1, D_WIDTH)),
            _const_spec((D_BLOCKS, D_BLOCK, D_BLOCK)),
            _const_spec((1, D_WIDTH)),
            _const_spec((D_BLOCKS, D_BLOCK, D_BLOCK)),
            _const_spec((1, D_WIDTH)),
            _const_spec((1, D_WIDTH)),
        ],
        out_specs=pl.BlockSpec((1, ts, D_WIDTH), lambda b, i: (b, i, 0)),
        out_shape=jax.ShapeDtypeStruct((batch, seq_len, D_WIDTH), BF16),
        scratch_shapes=[
            pltpu.VMEM((ts + SUBLANES, D_WIDTH), F32),
            pltpu.VMEM((ts, D_WIDTH), F32),
            pltpu.VMEM((ts, D_WIDTH), F32),
            pltpu.VMEM((ts, D_WIDTH), F32),
            pltpu.VMEM((SUBLANES, D_WIDTH), F32),
        ],
        compiler_params=_params("parallel", "arbitrary"),
        name="d_main",
    )(x3, vec(gain), w_in.astype(BF16), conv_w, vec(conv_b), w_rg.astype(BF16), vec(b_rg),
      w_ig.astype(BF16), vec(b_ig), vec(lru_param))


def _mixer_d(x, gain, batch, seq_len, w_in, conv_w, conv_b, w_rg, b_rg, w_ig, b_ig, lru_param, w_out):
    t = x.shape[0]
    y = _d_main(x.reshape(batch, seq_len, D_MODEL), gain, w_in, conv_w, conv_b, w_rg, b_rg, w_ig, b_ig, lru_param)
    return _proj_res(x, y.reshape(t, D_WIDTH), w_out)


def kernel(x, positions, norm_gains, ffn_w_in, ffn_w_out, a_w_in, a_q_gain, a_k_gain, a_w_out, b_w_in, b_q_a_gain, b_w_q_up, b_kv_a_gain, b_w_kv_up, b_q_gain, b_k_gain, b_w_out, c_w_in, c_w_out, d_w_in, d_conv_w, d_conv_b, d_w_rg, d_b_rg, d_w_ig, d_b_ig, d_lru_param, d_w_out):
    batch, seq_len, _ = x.shape
    depth = norm_gains.shape[0]
    t = batch * seq_len
    h = x.reshape(t, D_MODEL)
    trig_a = _rope_trig(positions, A_HEAD_DIM // 2)
    trig_b = _rope_trig(positions, B_ROPE // 2)
    trig_c = _rope_trig(positions, C_KDIM // 2)
    for i in range(depth):
        m, j = i % 4, i // 4
        h = _ffn(h, norm_gains[i, 0], ffn_w_in[i, 0], ffn_w_out[i, 0])
        g = norm_gains[i, 1]
        if m == 0:
            h = _mixer_a(h, g, batch, seq_len, trig_a, a_w_in[j], a_q_gain[j], a_k_gain[j], a_w_out[j])
        elif m == 1:
            h = _mixer_b(h, g, batch, seq_len, trig_b, b_w_in[j], b_q_a_gain[j], b_w_q_up[j], b_kv_a_gain[j],
                         b_w_kv_up[j], b_q_gain[j], b_k_gain[j], b_w_out[j])
        elif m == 2:
            h = _mixer_c(h, g, batch, seq_len, trig_c, c_w_in[j], c_w_out[j])
        else:
            h = _mixer_d(h, g, batch, seq_len, d_w_in[j], d_conv_w[j], d_conv_b[j], d_w_rg[j], d_b_rg[j],
                         d_w_ig[j], d_b_ig[j], d_lru_param[j], d_w_out[j])
        h = _ffn(h, norm_gains[i, 2], ffn_w_in[i, 1], ffn_w_out[i, 1])
    return h.reshape(batch, seq_len, D_MODEL)
```

```python
import functools
import math

import jax
import jax.numpy as jnp
from jax import lax
from jax.experimental import pallas as pl
from jax.experimental.pallas import tpu as pltpu

F32 = jnp.float32
BF16 = jnp.bfloat16

D_MODEL = 1024
D_FF = 2816
NORM_EPS = 1e-6
GN_EPS = 1e-5
ROPE_THETA = 10000.0
NEG_INF = -1e30
LOG2E = math.log2(math.e)
LN2 = math.log(2.0)

A_HEADS = 16
A_HEAD_DIM = 64
A_PATTERNS = ((128, 1), (512, 4), (2048, 16))
A_WINDOW_STEPS = 128
A_GROUP_WIDTH = A_HEADS * A_HEAD_DIM

B_HEADS = 16
B_NOPE = 64
B_ROPE = 32
B_QK = B_NOPE + B_ROPE
B_VDIM = 64
B_Q_RANK = 384
B_KV_RANK = 256

C_HEADS = 8
C_KDIM = 128
C_VDIM = 256
C_CHUNK = 256

D_WIDTH = 1024
D_BLOCKS = 4
D_BLOCK = D_WIDTH // D_BLOCKS
D_CONV = 4
LRU_C = 8.0

LANES = 128
SUBLANES = 8
VMEM_LIMIT_BYTES = 52 * 1024 * 1024

NT_DIMS = (((1,), (1,)), ((), ()))
TN_DIMS = (((0,), (0,)), ((), ()))


def _params(*semantics):
    return pltpu.CompilerParams(dimension_semantics=semantics, vmem_limit_bytes=VMEM_LIMIT_BYTES)


def _rms(x, g):
    ms = jnp.mean(x * x, axis=-1, keepdims=True)
    return x * lax.rsqrt(ms + NORM_EPS) * g


def _expm1(x):
    u = jnp.exp(x)
    near = u == 1.0
    ratio = (u - 1.0) * x / jnp.where(near, 1.0, jnp.log(u))
    return jnp.where(x < -1.0, u - 1.0, jnp.where(near, x, ratio))


def _const_spec(shape):
    nd = len(shape)
    return pl.BlockSpec(shape, lambda *_: (0,) * nd, pipeline_mode=pl.Buffered(1))


def _trig_body(pos_ref, inv_ref, cos_ref, sin_ref):
    ang = pos_ref[...] * inv_ref[...]
    cos_ref[...] = jnp.cos(ang)
    sin_ref[...] = jnp.sin(ang)


def _rope_trig(positions, half):
    t = positions.size
    per_row = LANES // half
    rows = t // per_row
    pos_rep = jnp.repeat(positions.reshape(rows, per_row).astype(F32), half, axis=1)
    inv = ROPE_THETA ** (-jnp.arange(half, dtype=F32) * 2.0 / (2 * half))
    inv_row = jnp.tile(inv, per_row)[None, :]
    tr = min(rows, 2048)
    cos, sin = pl.pallas_call(
        _trig_body,
        grid=(rows // tr,),
        in_specs=[pl.BlockSpec((tr, LANES), lambda i: (i, 0)), pl.BlockSpec((1, LANES), lambda i: (0, 0))],
        out_specs=[pl.BlockSpec((tr, LANES), lambda i: (i, 0))] * 2,
        out_shape=[jax.ShapeDtypeStruct((rows, LANES), F32)] * 2,
        compiler_params=_params("parallel"),
        name="rope_trig",
    )(pos_rep, inv_row)
    return cos.reshape(t, half), sin.reshape(t, half)


def _rope_gain_rows(gain_row, shift_a, shift_b, scale):
    return jnp.stack([gain_row, jnp.roll(gain_row, shift_a), jnp.roll(gain_row, shift_b)]) * scale


FFN_TM = 512
FFN_TF = 256


def _ffn_body(x_ref, g_ref, wg_ref, wu_ref, wo_ref, o_ref, act_ref):
    x = x_ref[...]
    xn = _rms(x, g_ref[...]).astype(BF16)
    for j in range(D_FF // FFN_TF):
        sl = slice(j * FFN_TF, (j + 1) * FFN_TF)
        gate = jnp.dot(xn, wg_ref[:, sl], preferred_element_type=F32)
        up = jnp.dot(xn, wu_ref[:, sl], preferred_element_type=F32)
        act_ref[:, sl] = (gate * jax.nn.sigmoid(gate) * up).astype(BF16)
    y = jnp.dot(act_ref[...], wo_ref[...], preferred_element_type=F32)
    o_ref[...] = x + 0.5 * y


def _ffn(x, gain, w_in, w_out):
    t = x.shape[0]
    tm = min(FFN_TM, t)
    wg = w_in[:, :D_FF].astype(BF16)
    wu = w_in[:, D_FF:].astype(BF16)
    wo = w_out.astype(BF16)
    return pl.pallas_call(
        _ffn_body,
        grid=(t // tm,),
        in_specs=[
            pl.BlockSpec((tm, D_MODEL), lambda i: (i, 0)),
            _const_spec((1, D_MODEL)),
            _const_spec((D_MODEL, D_FF)),
            _const_spec((D_MODEL, D_FF)),
            _const_spec((D_FF, D_MODEL)),
        ],
        out_specs=pl.BlockSpec((tm, D_MODEL), lambda i: (i, 0)),
        out_shape=jax.ShapeDtypeStruct((t, D_MODEL), F32),
        scratch_shapes=[pltpu.VMEM((tm, D_FF), BF16)],
        compiler_params=_params("parallel"),
        name="ffn",
    )(x, gain[None, :], wg, wu, wo)


PROJ_TM = 512


def _proj_res_body(x_ref, a_ref, w_ref, o_ref):
    o_ref[...] = x_ref[...] + jnp.dot(a_ref[...], w_ref[...], preferred_element_type=F32)


def _proj_res(x, a, w):
    t, k = a.shape
    tm = min(PROJ_TM, t)
    row = lambda i: (i, 0)
    return pl.pallas_call(
        _proj_res_body,
        grid=(t // tm,),
        in_specs=[pl.BlockSpec((tm, D_MODEL), row), pl.BlockSpec((tm, k), row), _const_spec((k, D_MODEL))],
        out_specs=pl.BlockSpec((tm, D_MODEL), row),
        out_shape=jax.ShapeDtypeStruct((t, D_MODEL), F32),
        compiler_params=_params("parallel"),
        name="proj_res",
    )(x, a, w.astype(BF16))


A_TM = 512
A_TL = 512
A_QSCALE = A_HEAD_DIM ** -0.5 * LOG2E


def _a_proj_body(x_ref, g_ref, w_ref, gain_ref, seg_ref, c_ref, sa_ref, sb_ref, o_ref, xn_ref):
    j = pl.program_id(1)
    width = w_ref.shape[1]

    @pl.when(j == 0)
    def _():
        xn_ref[...] = _rms(x_ref[...], g_ref[...]).astype(BF16)

    xn = xn_ref[...]

    @pl.when(j < 2)
    def _():
        tab_c = c_ref[...] * gain_ref[0, 0:1, :]
        tab_a = sa_ref[...] * gain_ref[0, 1:2, :]
        tab_b = sb_ref[...] * gain_ref[0, 2:3, :]
        for c in range(width // 256):
            y2 = jnp.dot(xn, w_ref[:, c * 256:(c + 1) * 256], preferred_element_type=F32)
            ss = jnp.dot((y2 * y2).astype(BF16), seg_ref[...], preferred_element_type=F32)
            inv = lax.rsqrt(ss + A_HEAD_DIM * NORM_EPS)
            for hh in range(2):
                y = y2[:, hh * LANES:(hh + 1) * LANES]
                rot = y * tab_c + pltpu.roll(y, 96, 1) * tab_a + pltpu.roll(y, 32, 1) * tab_b
                lo = c * 256 + hh * LANES
                o_ref[0, :, lo:lo + LANES] = (rot * inv[:, hh * LANES:(hh + 1) * LANES]).astype(BF16)

    @pl.when(j == 2)
    def _():
        for c in range(width // 256):
            sl = slice(c * 256, (c + 1) * 256)
            o_ref[0, :, sl] = jnp.dot(xn, w_ref[:, sl], preferred_element_type=F32).astype(BF16)


def _a_proj(x, gain, w_in, q_gain, k_gain, cos, sin_a, sin_b):
    t = x.shape[0]
    tm = min(A_TM, t)
    width = 3 * A_GROUP_WIDTH
    root = A_HEAD_DIM ** 0.5
    gains = jnp.stack([_rope_gain_rows(jnp.tile(q_gain, 2), 96, 32, root * A_QSCALE),
                       _rope_gain_rows(jnp.tile(k_gain, 2), 96, 32, root)])
    seg = jnp.kron(jnp.eye(256 // A_HEAD_DIM, dtype=F32), jnp.ones((A_HEAD_DIM, A_HEAD_DIM), F32)).astype(BF16)
    tab = pl.BlockSpec((tm, LANES), lambda i, j: (i, 0))
    return pl.pallas_call(
        _a_proj_body,
        grid=(t // tm, 3),
        in_specs=[
            pl.BlockSpec((tm, D_MODEL), lambda i, j: (i, 0)),
            pl.BlockSpec((1, D_MODEL), lambda i, j: (0, 0)),
            pl.BlockSpec((D_MODEL, width), lambda i, j: (0, j)),
            pl.BlockSpec((1, 3, LANES), lambda i, j: (jnp.minimum(j, 1), 0, 0)),
            pl.BlockSpec((256, 256), lambda i, j: (0, 0)),
            tab, tab, tab,
        ],
        out_specs=pl.BlockSpec((1, tm, width), lambda i, j: (j, i, 0)),
        out_shape=jax.ShapeDtypeStruct((3, t, width), BF16),
        scratch_shapes=[pltpu.VMEM((tm, D_MODEL), BF16)],
        compiler_params=_params("parallel", "arbitrary"),
        name="a_proj",
    )(x, gain[None, :], w_in.astype(BF16), gains, seg, cos, sin_a, sin_b)


def _a_attn_body(q_ref, kc_ref, kp_ref, vc_ref, vp_ref, o_ref, m_ref, l_ref, *, whole_seq):
    li = pl.program_id(2)
    w = A_WINDOW_STEPS
    nblk = q_ref.shape[2] // w
    even = lax.broadcasted_iota(jnp.int32, (w, LANES), 1) < A_HEAD_DIM
    key = lax.broadcasted_iota(jnp.int32, (2 * w, 2 * w), 0)
    col = lax.broadcasted_iota(jnp.int32, (2 * w, 2 * w), 1)
    qry = jnp.where(col >= w, col - w, col)
    band_bias = jnp.where((key >= qry) & (key <= qry + w), 0.0, NEG_INF)
    if whole_seq:
        key1 = lax.broadcasted_iota(jnp.int32, (w, 2 * w), 0)
        col1 = lax.broadcasted_iota(jnp.int32, (w, 2 * w), 1)
        tri_bias = jnp.where(key1 <= jnp.where(col1 >= w, col1 - w, col1), 0.0, NEG_INF)
    else:
        is_first = jnp.where(li == 0, 1.0, 0.0)
        start_bias = band_bias + jnp.where(key < w, NEG_INF, 0.0) * is_first

    for hp in range(A_HEADS // 2):
        ls = slice(hp * LANES, (hp + 1) * LANES)
        for j in range(nblk):
            rs = slice(j * w, (j + 1) * w)
            qb = q_ref[0, 0, rs, ls]
            zero = jnp.zeros_like(qb)
            q2 = jnp.concatenate([jnp.where(even, qb, zero), jnp.where(even, zero, qb)], axis=0)
            if j == 0 and whole_seq:
                kk, vv, bias = kc_ref[0, 0, rs, ls], vc_ref[0, 0, rs, ls], tri_bias
            elif j == 0:
                kk = jnp.concatenate([kp_ref[0, 0, :, ls], kc_ref[0, 0, rs, ls]], axis=0)
                vv = jnp.concatenate([vp_ref[0, 0, :, ls], vc_ref[0, 0, rs, ls]], axis=0)
                bias = start_bias
            else:
                kk = kc_ref[0, 0, (j - 1) * w:(j + 1) * w, ls]
                vv = vc_ref[0, 0, (j - 1) * w:(j + 1) * w, ls]
                bias = band_bias
            st = lax.dot_general(kk, q2, NT_DIMS, preferred_element_type=F32) + bias
            probs = []
            for hh in range(2):
                sh = st[:, hh * w:(hh + 1) * w]
                m = jnp.max(sh, axis=0, keepdims=True)
                ph = jnp.exp2(sh - m)
                m_ref[0, 0, hp, hh:hh + 1, rs] = m
                l_ref[0, 0, hp, hh:hh + 1, rs] = jnp.sum(ph, axis=0, keepdims=True)
                probs.append(ph.astype(BF16))
            p = jnp.concatenate(probs, axis=1)
            o2 = lax.dot_general(p, vv, TN_DIMS, preferred_element_type=F32)
            o_ref[0, rs, ls] = jnp.where(even, o2[:w], o2[w:]).astype(BF16)


def _a_attn(qkv, group, batch, seq_len):
    _, d = A_PATTERNS[group]
    sub_len = seq_len // d
    tl = min(A_TL, sub_len)
    w = A_WINDOW_STEPS
    ngroups = len(A_PATTERNS)
    view = qkv.reshape(3, batch, sub_len, d * ngroups * A_GROUP_WIDTH)
    cur = lambda which: pl.BlockSpec(
        (1, 1, tl, A_GROUP_WIDTH), lambda b, r, li: (which, b, li, r * ngroups + group))
    prev = lambda which: pl.BlockSpec(
        (1, 1, w, A_GROUP_WIDTH), lambda b, r, li: (which, b, jnp.maximum(li * (tl // w) - 1, 0), r * ngroups + group))
    stat = pl.BlockSpec((1, 1, A_HEADS // 2, 2, tl), lambda b, r, li: (b, r, 0, 0, li))
    o, m, l = pl.pallas_call(
        functools.partial(_a_attn_body, whole_seq=(tl == sub_len)),
        grid=(batch, d, sub_len // tl),
        in_specs=[cur(0), cur(1), prev(1), cur(2), prev(2)],
        out_specs=[pl.BlockSpec((1, tl, A_GROUP_WIDTH), lambda b, r, li: (b, li, r)), stat, stat],
        out_shape=[
            jax.ShapeDtypeStruct((batch, sub_len, d * A_GROUP_WIDTH), BF16),
            jax.ShapeDtypeStruct((batch, d, A_HEADS // 2, 2, sub_len), F32),
            jax.ShapeDtypeStruct((batch, d, A_HEADS // 2, 2, sub_len), F32),
        ],
        compiler_params=_params("parallel", "parallel", "arbitrary"),
        name="a_attn",
    )(view, view, view, view, view)
    return o.reshape(batch * seq_len, A_GROUP_WIDTH), m, l


def _a_out_body(x_ref, o0_ref, o1_ref, o2_ref, m_ref, l_ref, e_ref, w_ref, out_ref):
    m0, m1, m2 = m_ref[0], m_ref[1], m_ref[2]
    top = jnp.maximum(jnp.maximum(m0, m1), m2)
    c0, c1, c2 = jnp.exp2(m0 - top), jnp.exp2(m1 - top), jnp.exp2(m2 - top)
    inv = 1.0 / (c0 * l_ref[0] + c1 * l_ref[1] + c2 * l_ref[2])
    merged = None
    for coef, o_ref in ((c0, o0_ref), (c1, o1_ref), (c2, o2_ref)):
        wt = coef * inv
        hi = wt.astype(BF16)
        lo = (wt - hi.astype(F32)).astype(BF16)
        wexp = (jnp.dot(hi, e_ref[...], preferred_element_type=F32)
                + jnp.dot(lo, e_ref[...], preferred_element_type=F32))
        term = wexp * o_ref[...].astype(F32)
        merged = term if merged is None else merged + term
    out_ref[...] = x_ref[...] + jnp.dot(merged.astype(BF16), w_ref[...], preferred_element_type=F32)


def _a_out(x, outs, m3, l3, w_out):
    t = x.shape[0]
    tm = min(PROJ_TM, t)
    expand = jnp.repeat(jnp.eye(A_HEADS, dtype=BF16), A_HEAD_DIM, axis=1)
    row = lambda i: (i, 0)
    o_spec = pl.BlockSpec((tm, A_GROUP_WIDTH), row)
    s_spec = pl.BlockSpec((3, tm, A_HEADS), lambda i: (0, i, 0))
    return pl.pallas_call(
        _a_out_body,
        grid=(t // tm,),
        in_specs=[pl.BlockSpec((tm, D_MODEL), row), o_spec, o_spec, o_spec, s_spec, s_spec,
                  _const_spec((A_HEADS, A_GROUP_WIDTH)), _const_spec((A_GROUP_WIDTH, D_MODEL))],
        out_specs=pl.BlockSpec((tm, D_MODEL), row),
        out_shape=jax.ShapeDtypeStruct((t, D_MODEL), F32),
        compiler_params=_params("parallel"),
        name="a_out",
    )(x, *outs, m3, l3, expand, w_out.astype(BF16))


def _mixer_a(x, gain, batch, seq_len, trig, w_in, q_gain, k_gain, w_out):
    t = x.shape[0]
    cos, sin = trig
    zero = jnp.zeros_like(sin)
    tab_c = jnp.tile(jnp.concatenate([cos, cos], axis=1), (1, 2))
    tab_sa = jnp.tile(jnp.concatenate([-sin, zero], axis=1), (1, 2))
    tab_sb = jnp.tile(jnp.concatenate([zero, sin], axis=1), (1, 2))
    qkv = _a_proj(x, gain, w_in, q_gain, k_gain, tab_c, tab_sa, tab_sb)
    outs, ms, ls = [], [], []
    for gi in range(len(A_PATTERNS)):
        o, m, l = _a_attn(qkv, gi, batch, seq_len)
        outs.append(o)
        ms.append(m.transpose(0, 4, 1, 2, 3).reshape(t, A_HEADS))
        ls.append(l.transpose(0, 4, 1, 2, 3).reshape(t, A_HEADS))
    return _a_out(x, outs, jnp.stack(ms), jnp.stack(ls), w_out)


B_TM = 512
B_TQ = 256
B_TK = 256
B_HEAD_PAD = LANES
B_IN_PAD = 768
B_QSCALE = B_QK ** -0.5 * LOG2E


def _b_proj_body(x_ref, g_ref, win_ref, qag_ref, kvag_ref, wq_ref, wkv_ref, qg_ref, kg_ref, seg_ref,
                 c_ref, sa_ref, sb_ref, q_ref, k_ref, vt_ref):
    tm = x_ref.shape[0]
    xn = _rms(x_ref[...], g_ref[...]).astype(BF16)
    h = jnp.dot(xn, win_ref[...], preferred_element_type=F32)
    c_q = h[:, :B_Q_RANK]
    rest = h[:, B_Q_RANK:]
    cq = (_rms(c_q, qag_ref[...])).astype(BF16)
    lane_r = lax.broadcasted_iota(jnp.int32, rest.shape, 1)
    is_kv = lane_r < B_KV_RANK
    ms = jnp.sum(jnp.where(is_kv, rest * rest, 0.0), axis=-1, keepdims=True) * (1.0 / B_KV_RANK)
    ckv = jnp.where(is_kv, rest * lax.rsqrt(ms + NORM_EPS) * kvag_ref[...], rest).astype(BF16)
    cos, sin_a, sin_b = c_ref[...], sa_ref[...], sb_ref[...]
    q_tabs = (cos * qg_ref[0:1, :], sin_a * qg_ref[1:2, :], sin_b * qg_ref[2:3, :])
    k_tabs = (cos * kg_ref[0:1, :], sin_a * kg_ref[1:2, :], sin_b * kg_ref[2:3, :])

    def heads_norm_rope(y2, tabs, out_ref, lo):
        ss = jnp.dot((y2 * y2).astype(BF16), seg_ref[...], preferred_element_type=F32)
        inv = lax.rsqrt(ss + B_QK * NORM_EPS)
        for hh in range(2):
            y = y2[:, hh * LANES:(hh + 1) * LANES]
            rot = (y * tabs[0] + pltpu.roll(y, LANES - B_ROPE // 2, 1) * tabs[1]
                   + pltpu.roll(y, B_ROPE // 2, 1) * tabs[2])
            out_ref[:, lo + hh * LANES:lo + (hh + 1) * LANES] = (
                rot * inv[:, hh * LANES:(hh + 1) * LANES]).astype(BF16)

    for hd in range(B_HEADS // 2):
        sl = slice(hd * 256, (hd + 1) * 256)
        heads_norm_rope(jnp.dot(cq, wq_ref[:, sl], preferred_element_type=F32), q_tabs, q_ref, hd * 256)
        heads_norm_rope(jnp.dot(ckv, wkv_ref[:, sl], preferred_element_type=F32), k_tabs, k_ref, hd * 256)
    k_width = B_HEADS * B_HEAD_PAD
    for c in range(B_HEADS * B_VDIM // 256):
        sl = slice(k_width + c * 256, k_width + (c + 1) * 256)
        vc = jnp.dot(ckv, wkv_ref[:, sl], preferred_element_type=F32).astype(BF16)
        for s in range(tm // B_TK):
            vt_ref[0, s, c * 256:(c + 1) * 256, :] = vc[s * B_TK:(s + 1) * B_TK, :].T


def _b_proj(x, gain, batch, seq_len, w_in, q_a_gain, w_q_up, kv_a_gain, w_kv_up, q_gain, k_gain, cos, sin_a, sin_b):
    t = x.shape[0]
    tm = min(B_TM, seq_len)
    in_w = B_Q_RANK + B_KV_RANK + B_ROPE
    win = jnp.pad(w_in, ((0, 0), (0, B_IN_PAD - in_w))).astype(BF16)
    wq = jnp.pad(w_q_up.reshape(B_Q_RANK, B_HEADS, B_QK), ((0, 0), (0, 0), (0, B_HEAD_PAD - B_QK)))
    wq = wq.reshape(B_Q_RANK, B_HEADS * B_HEAD_PAD).astype(BF16)
    kv_in = B_IN_PAD - B_Q_RANK
    wkv = w_kv_up.reshape(B_KV_RANK, B_HEADS, B_NOPE + B_VDIM)
    wk = jnp.zeros((kv_in, B_HEADS, B_HEAD_PAD), F32)
    wk = wk.at[:B_KV_RANK, :, :B_NOPE].set(wkv[:, :, :B_NOPE])
    wk = wk.at[B_KV_RANK:B_KV_RANK + B_ROPE, :, B_NOPE:B_QK].set(
        jnp.broadcast_to(jnp.eye(B_ROPE, dtype=F32)[:, None, :], (B_ROPE, B_HEADS, B_ROPE)))
    wv = jnp.zeros((kv_in, B_HEADS, B_VDIM), F32).at[:B_KV_RANK].set(wkv[:, :, B_NOPE:])
    wkv_full = jnp.concatenate(
        [wk.reshape(kv_in, B_HEADS * B_HEAD_PAD), wv.reshape(kv_in, B_HEADS * B_VDIM)], axis=1).astype(BF16)
    root = B_QK ** 0.5
    shift_a, shift_b = LANES - B_ROPE // 2, B_ROPE // 2
    pad_gain = lambda gn: jnp.pad(gn, (0, B_HEAD_PAD - B_QK))
    q_rows = _rope_gain_rows(pad_gain(q_gain), shift_a, shift_b, root * B_QSCALE)
    k_rows = _rope_gain_rows(pad_gain(k_gain), shift_a, shift_b, root)
    kvag = jnp.pad(kv_a_gain, (0, kv_in - B_KV_RANK))[None, :]
    seg = jnp.kron(jnp.eye(2, dtype=F32), jnp.ones((B_HEAD_PAD, B_HEAD_PAD), F32)).astype(BF16)
    row = lambda i: (i, 0)
    tab = pl.BlockSpec((tm, LANES), row)
    tiles = seq_len // tm
    return pl.pallas_call(
        _b_proj_body,
        grid=(t // tm,),
        in_specs=[
            pl.BlockSpec((tm, D_MODEL), row),
            _const_spec((1, D_MODEL)),
            _const_spec((D_MODEL, B_IN_PAD)),
            _const_spec((1, B_Q_RANK)),
            _const_spec((1, kv_in)),
            _const_spec((B_Q_RANK, B_HEADS * B_HEAD_PAD)),
            _const_spec((kv_in, B_HEADS * (B_HEAD_PAD + B_VDIM))),
            _const_spec((3, B_HEAD_PAD)),
            _const_spec((3, B_HEAD_PAD)),
            _const_spec((2 * B_HEAD_PAD, 2 * B_HEAD_PAD)),
            tab, tab, tab,
        ],
        out_specs=[
            pl.BlockSpec((tm, B_HEADS * B_HEAD_PAD), row),
            pl.BlockSpec((tm, B_HEADS * B_HEAD_PAD), row),
            pl.BlockSpec((1, tm // B_TK, B_HEADS * B_VDIM, B_TK), lambda i: (i // tiles, i % tiles, 0, 0)),
        ],
        out_shape=[
            jax.ShapeDtypeStruct((t, B_HEADS * B_HEAD_PAD), BF16),
            jax.ShapeDtypeStruct((t, B_HEADS * B_HEAD_PAD), BF16),
            jax.ShapeDtypeStruct((batch, seq_len // B_TK, B_HEADS * B_VDIM, B_TK), BF16),
        ],
        compiler_params=_params("parallel"),
        name="b_proj",
    )(x, gain[None, :], win, q_a_gain[None, :], kvag, wq, wkv_full, q_rows, k_rows, seg, cos, sin_a, sin_b)


def _b_attn_body(q_ref, k_ref, vt_ref, o_ref, m_ref, l_ref, acc_ref, ot_ref):
    seq_len = q_ref.shape[1]
    tq, tk = B_TQ, B_TK
    key = lax.broadcasted_iota(jnp.int32, (tk, tq), 0)
    qry = lax.broadcasted_iota(jnp.int32, (tk, tq), 1)
    diag_bias = jnp.where(key <= qry, 0.0, NEG_INF)
    n = seq_len // tq
    for kj in range(n):
        lo = kj * tq
        for hh in range(2):
            hl = slice(hh * LANES, (hh + 1) * LANES)
            k = k_ref[0, lo:lo + tk, hl]
            vt = vt_ref[0, kj, hh * B_VDIM:(hh + 1) * B_VDIM, :]
            st = lax.dot_general(k, q_ref[0, lo:, hl], NT_DIMS, preferred_element_type=F32)
            diag = st[:, :tq] + diag_bias
            st = diag if kj == n - 1 else jnp.concatenate([diag, st[:, tq:]], axis=1)
            blk_max = jnp.max(st, axis=0, keepdims=True)
            if kj == 0:
                m_new = blk_max
                p = jnp.exp2(st - m_new)
                l = jnp.sum(p, axis=0, keepdims=True)
                acc = jnp.dot(vt, p.astype(BF16), preferred_element_type=F32)
            else:
                m = m_ref[hh, :, lo:]
                m_new = jnp.maximum(m, blk_max)
                alpha = jnp.exp2(m - m_new)
                p = jnp.exp2(st - m_new)
                l = alpha * l_ref[hh, :, lo:] + jnp.sum(p, axis=0, keepdims=True)
                acc = alpha * acc_ref[hh, :, lo:] + jnp.dot(vt, p.astype(BF16), preferred_element_type=F32)
            m_ref[hh, :, lo:] = m_new
            l_ref[hh, :, lo:] = l
            acc_ref[hh, :, lo:] = acc
            ot_ref[hh * B_VDIM:(hh + 1) * B_VDIM, :] = acc_ref[hh, :, lo:lo + tq] * (1.0 / l_ref[hh, :, lo:lo + tq])
        o_ref[0, lo:lo + tq, :] = ot_ref[...].T.astype(BF16)


def _b_attn(q, k, vt, batch, seq_len):
    return pl.pallas_call(
        _b_attn_body,
        grid=(batch, B_HEADS // 2),
        in_specs=[
            pl.BlockSpec((1, seq_len, 2 * B_HEAD_PAD), lambda b, hp: (b, 0, hp)),
            pl.BlockSpec((1, seq_len, 2 * B_HEAD_PAD), lambda b, hp: (b, 0, hp)),
            pl.BlockSpec((1, seq_len // B_TK, 2 * B_VDIM, B_TK), lambda b, hp: (b, 0, hp, 0)),
        ],
        out_specs=pl.BlockSpec((1, seq_len, 2 * B_VDIM), lambda b, hp: (b, 0, hp)),
        out_shape=jax.ShapeDtypeStruct((batch, seq_len, B_HEADS * B_VDIM), BF16),
        scratch_shapes=[
            pltpu.VMEM((2, 1, seq_len), F32),
            pltpu.VMEM((2, 1, seq_len), F32),
            pltpu.VMEM((2, B_VDIM, seq_len), F32),
            pltpu.VMEM((2 * B_VDIM, B_TQ), F32),
        ],
        compiler_params=_params("parallel", "parallel"),
        name="b_attn",
    )(q, k, vt)


def _mixer_b(x, gain, batch, seq_len, trig, w_in, q_a_gain, w_q_up, kv_a_gain, w_kv_up, q_gain, k_gain, w_out):
    t = x.shape[0]
    cos, sin = trig
    one = jnp.ones((t, B_NOPE), F32)
    zero64 = jnp.zeros((t, B_NOPE), F32)
    z16 = jnp.zeros_like(sin)
    tail = jnp.zeros((t, B_HEAD_PAD - B_QK), F32)
    tab_c = jnp.concatenate([one, cos, cos, tail + 1.0], axis=1)
    tab_sa = jnp.concatenate([zero64, -sin, z16, tail], axis=1)
    tab_sb = jnp.concatenate([zero64, z16, sin, tail], axis=1)
    q, k, vt = _b_proj(x, gain, batch, seq_len, w_in, q_a_gain, w_q_up, kv_a_gain, w_kv_up, q_gain, k_gain,
                       tab_c, tab_sa, tab_sb)
    shp = lambda a: a.reshape(batch, seq_len, a.shape[-1])
    o = _b_attn(shp(q), shp(k), vt, batch, seq_len)
    return _proj_res(x, o.reshape(t, B_HEADS * B_VDIM), w_out)


C_TM = 512
C_QK_W = C_HEADS * C_KDIM
C_V_W = C_HEADS * C_VDIM


def _c_proj_body(x_ref, g_ref, w_ref, c_ref, s_ref, o_ref, xn_ref):
    j = pl.program_id(1)
    width = w_ref.shape[1]

    @pl.when(j == 0)
    def _():
        xn_ref[...] = _rms(x_ref[...], g_ref[...]).astype(BF16)

    xn = xn_ref[...]

    @pl.when(j == 0)
    def _():
        cos, sin = c_ref[...], s_ref[...]
        for c in range(width // 256):
            y2 = jnp.dot(xn, w_ref[:, c * 256:(c + 1) * 256], preferred_element_type=F32)
            for hh in range(2):
                y = y2[:, hh * LANES:(hh + 1) * LANES]
                out = y * cos + pltpu.roll(y, C_KDIM // 2, 1) * sin
                lo = c * 256 + hh * LANES
                if lo >= C_QK_W:
                    out = out * (C_KDIM ** -0.5)
                o_ref[0, :, lo:lo + LANES] = out.astype(BF16)

    @pl.when(j == 1)
    def _():
        for c in range(width // 256):
            sl = slice(c * 256, (c + 1) * 256)
            o_ref[0, :, sl] = jnp.dot(xn, w_ref[:, sl], preferred_element_type=F32).astype(BF16)

    @pl.when(j == 2)
    def _():
        for c in range(width // 256):
            sl = slice(c * 256, (c + 1) * 256)
            gt = jnp.dot(xn, w_ref[:, sl], preferred_element_type=F32)
            o_ref[0, :, sl] = (gt * jax.nn.sigmoid(gt)).astype(BF16)


def _c_proj(x, gain, w_in, cos, sin):
    t = x.shape[0]
    tm = min(C_TM, t)
    width = 2 * C_QK_W
    tab = pl.BlockSpec((tm, LANES), lambda i, j: (i, 0))
    return pl.pallas_call(
        _c_proj_body,
        grid=(t // tm, 3),
        in_specs=[
            pl.BlockSpec((tm, D_MODEL), lambda i, j: (i, 0)),
            pl.BlockSpec((1, D_MODEL), lambda i, j: (0, 0)),
            pl.BlockSpec((D_MODEL, width), lambda i, j: (0, j)),
            tab, tab,
        ],
        out_specs=pl.BlockSpec((1, tm, width), lambda i, j: (j, i, 0)),
        out_shape=jax.ShapeDtypeStruct((3, t, width), BF16),
        scratch_shapes=[pltpu.VMEM((tm, D_MODEL), BF16)],
        compiler_params=_params("parallel", "arbitrary"),
        name="c_proj",
    )(x, gain[None, :], w_in.astype(BF16), cos, sin)


def _c_ret_body(q_ref, k_ref, v_ref, dec_ref, xi_ref, zeta_ref, cd_ref, y_ref, r_ref):
    n_chunks = q_ref.shape[2] // C_CHUNK
    r_ref[...] = jnp.zeros_like(r_ref)
    decay = dec_ref[0]
    xi = xi_ref[0]
    zeta = zeta_ref[0]
    cd = cd_ref[0]

    def step(n, carry):
        rows = pl.ds(pl.multiple_of(n * C_CHUNK, C_CHUNK), C_CHUNK)
        qc = q_ref[0, 0, rows, :]
        kc = k_ref[0, 0, rows, :]
        vc = v_ref[0, 0, rows, :]
        s = lax.dot_general(qc, kc, NT_DIMS, preferred_element_type=F32) * decay
        inner = jnp.dot(s.astype(BF16), vc, preferred_element_type=F32)
        r_old = r_ref[...]
        cross = jnp.dot(qc, r_old.astype(BF16), preferred_element_type=F32) * xi
        kz = (kc.astype(F32) * zeta).astype(BF16)
        r_ref[...] = cd * r_old + lax.dot_general(kz, vc, TN_DIMS, preferred_element_type=F32)
        y = inner + cross
        mu = jnp.mean(y, axis=-1, keepdims=True)
        yc = y - mu
        var = jnp.mean(yc * yc, axis=-1, keepdims=True)
        y_ref[0, rows, :] = (yc * lax.rsqrt(var + GN_EPS)).astype(BF16)
        return carry

    lax.fori_loop(0, n_chunks, step, 0)


def _c_ret(qkv, batch, seq_len):
    cc = C_CHUNK
    log_g = jnp.log(1.0 - 2.0 ** (-5.0 - jnp.arange(C_HEADS, dtype=F32)))
    idx = jnp.arange(cc, dtype=F32)
    diff = idx[:, None] - idx[None, :]
    causal = diff >= 0
    decay = jnp.where(causal[None], jnp.exp(jnp.where(causal, diff, 0.0)[None] * log_g[:, None, None]), 0.0)
    xi = jnp.exp((idx + 1.0)[None, :] * log_g[:, None])[:, :, None]
    zeta = jnp.exp((cc - 1.0 - idx)[None, :] * log_g[:, None])[:, :, None]
    cdec = jnp.broadcast_to(jnp.exp(cc * log_g)[:, None, None], (C_HEADS, 1, C_VDIM))
    per_head = lambda shape: pl.BlockSpec((1,) + shape, lambda b, h: (h, 0, 0))
    return pl.pallas_call(
        _c_ret_body,
        grid=(batch, C_HEADS),
        in_specs=[
            pl.BlockSpec((1, 1, seq_len, C_KDIM), lambda b, h: (0, b, 0, h)),
            pl.BlockSpec((1, 1, seq_len, C_KDIM), lambda b, h: (0, b, 0, C_HEADS + h)),
            pl.BlockSpec((1, 1, seq_len, C_VDIM), lambda b, h: (1, b, 0, h)),
            per_head((cc, cc)), per_head((cc, 1)), per_head((cc, 1)), per_head((1, C_VDIM)),
        ],
        out_specs=pl.BlockSpec((1, seq_len, C_VDIM), lambda b, h: (b, 0, h)),
        out_shape=jax.ShapeDtypeStruct((batch, seq_len, C_V_W), BF16),
        scratch_shapes=[pltpu.VMEM((C_KDIM, C_VDIM), F32)],
        compiler_params=_params("parallel", "parallel"),
        name="c_ret",
    )(qkv, qkv, qkv, decay, xi, zeta, cdec)


def _c_out_body(x_ref, y_ref, gate_ref, w_ref, o_ref):
    a = (y_ref[...].astype(F32) * gate_ref[0].astype(F32)).astype(BF16)
    o_ref[...] = x_ref[...] + jnp.dot(a, w_ref[...], preferred_element_type=F32)


def _c_out(x, y, proj, w_out):
    t = x.shape[0]
    tm = min(PROJ_TM, t)
    row = lambda i: (i, 0)
    return pl.pallas_call(
        _c_out_body,
        grid=(t // tm,),
        in_specs=[pl.BlockSpec((tm, D_MODEL), row), pl.BlockSpec((tm, C_V_W), row),
                  pl.BlockSpec((1, tm, C_V_W), lambda i: (2, i, 0)), _const_spec((C_V_W, D_MODEL))],
        out_specs=pl.BlockSpec((tm, D_MODEL), row),
        out_shape=jax.ShapeDtypeStruct((t, D_MODEL), F32),
        compiler_params=_params("parallel"),
        name="c_out",
    )(x, y, proj, w_out.astype(BF16))


def _mixer_c(x, gain, batch, seq_len, trig, w_in, w_out):
    t = x.shape[0]
    cos, sin = trig
    tab_c = jnp.concatenate([cos, cos], axis=1)
    tab_s = jnp.concatenate([-sin, sin], axis=1)
    proj = _c_proj(x, gain, w_in, tab_c, tab_s)
    y = _c_ret(proj.reshape(3, batch, seq_len, 2 * C_QK_W), batch, seq_len)
    return _c_out(x, y.reshape(t, C_V_W), proj, w_out)


D_TS = 256


def _d_body(x_ref, g_ref, win_ref, cw_ref, cb_ref, wrg_ref, brg_ref, wig_ref, big_ref, lru_ref,
            o_ref, ubuf_ref, a_ref, b_ref, hs_ref, h_ref):
    ti = pl.program_id(1)
    ts = x_ref.shape[1]
    pad = SUBLANES

    @pl.when(ti == 0)
    def _():
        ubuf_ref[0:pad, :] = jnp.zeros((pad, D_WIDTH), F32)
        h_ref[...] = jnp.zeros_like(h_ref)

    xn = _rms(x_ref[0], g_ref[...]).astype(BF16)
    gate = jnp.dot(xn, win_ref[:, :D_WIDTH], preferred_element_type=F32)
    u = jnp.dot(xn, win_ref[:, D_WIDTH:], preferred_element_type=F32)
    ubuf_ref[pad:pad + ts, :] = u
    uc = cb_ref[...] + cw_ref[D_CONV - 1:D_CONV, :] * u
    for k in range(D_CONV - 1):
        off = pad - (D_CONV - 1) + k
        uc = uc + cw_ref[k:k + 1, :] * ubuf_ref[off:off + ts, :]
    ubuf_ref[0:pad, :] = ubuf_ref[ts:ts + pad, :]

    ucb = uc.astype(BF16)
    softplus_neg = jax.nn.softplus(-lru_ref[...])
    for n in range(D_BLOCKS):
        sl = slice(n * D_BLOCK, (n + 1) * D_BLOCK)
        r = jax.nn.sigmoid(jnp.dot(ucb[:, sl], wrg_ref[n], preferred_element_type=F32) + brg_ref[:, sl])
        ig = jax.nn.sigmoid(jnp.dot(ucb[:, sl], wig_ref[n], preferred_element_type=F32) + big_ref[:, sl])
        log_a = -LRU_C * r * softplus_neg[:, sl]
        a_ref[:, sl] = jnp.exp(log_a)
        b_ref[:, sl] = jnp.sqrt(jnp.maximum(-_expm1(2.0 * log_a), 0.0)) * (ig * uc[:, sl])

    row = lax.broadcasted_iota(jnp.int32, (SUBLANES, D_WIDTH), 0)

    def scan_group(j, h):
        rows = pl.ds(pl.multiple_of(j * SUBLANES, SUBLANES), SUBLANES)
        a = a_ref[rows, :]
        b = b_ref[rows, :]
        for s in (1, 2, 4):
            keep = row >= s
            b = jnp.where(keep, a * pltpu.roll(b, s, 0) + b, b)
            a = jnp.where(keep, a * pltpu.roll(a, s, 0), a)
        hs = a * h + b
        hs_ref[rows, :] = hs
        return jnp.broadcast_to(hs[SUBLANES - 1:SUBLANES, :], (SUBLANES, D_WIDTH))

    h_ref[...] = lax.fori_loop(0, ts // SUBLANES, scan_group, h_ref[...])
    o_ref[0] = (jax.nn.gelu(gate) * hs_ref[...]).astype(BF16)


def _d_main(x3, gain, w_in, conv_w, conv_b, w_rg, b_rg, w_ig, b_ig, lru_param):
    batch, seq_len, _ = x3.shape
    ts = min(D_TS, seq_len)
    vec = lambda a: a[None, :]
    return pl.pallas_call(
        _d_body,
        grid=(batch, seq_len // ts),
        in_specs=[
            pl.BlockSpec((1, ts, D_MODEL), lambda b, i: (b, i, 0)),
            _const_spec((1, D_MODEL)),
            _const_spec((D_MODEL, 2 * D_WIDTH)),
            _const_spec((D_CONV, D_WIDTH)),
            _const_spec((1, D_WIDTH)),
            _const_spec((D_BLOCKS, D_BLOCK, D_BLOCK)),
            _const_spec((1, D_WIDTH)),
            _const_spec((D_BLOCKS, D_BLOCK, D_BLOCK)),
            _const_spec((1, D_WIDTH)),
            _const_spec((1, D_WIDTH)),
        ],
        out_specs=pl.BlockSpec((1, ts, D_WIDTH), lambda b, i: (b, i, 0)),
        out_shape=jax.ShapeDtypeStruct((batch, seq_len, D_WIDTH), BF16),
        scratch_shapes=[
            pltpu.VMEM((ts + SUBLANES, D_WIDTH), F32),
            pltpu.VMEM((ts, D_WIDTH), F32),
            pltpu.VMEM((ts, D_WIDTH), F32),
            pltpu.VMEM((ts, D_WIDTH), F32),
            pltpu.VMEM((SUBLANES, D_WIDTH), F32),
        ],
        compiler_params=_params("parallel", "arbitrary"),
        name="d_main",
    )(x3, vec(gain), w_in.astype(BF16), conv_w, vec(conv_b), w_rg.astype(BF16), vec(b_rg),
      w_ig.astype(BF16), vec(b_ig), vec(lru_param))


def _mixer_d(x, gain, batch, seq_len, w_in, conv_w, conv_b, w_rg, b_rg, w_ig, b_ig, lru_param, w_out):
    t = x.shape[0]
    y = _d_main(x.reshape(batch, seq_len, D_MODEL), gain, w_in, conv_w, conv_b, w_rg, b_rg, w_ig, b_ig, lru_param)
    return _proj_res(x, y.reshape(t, D_WIDTH), w_out)


def kernel(x, positions, norm_gains, ffn_w_in, ffn_w_out, a_w_in, a_q_gain, a_k_gain, a_w_out, b_w_in, b_q_a_gain, b_w_q_up, b_kv_a_gain, b_w_kv_up, b_q_gain, b_k_gain, b_w_out, c_w_in, c_w_out, d_w_in, d_conv_w, d_conv_b, d_w_rg, d_b_rg, d_w_ig, d_b_ig, d_lru_param, d_w_out):
    batch, seq_len, _ = x.shape
    depth = norm_gains.shape[0]
    t = batch * seq_len
    h = x.reshape(t, D_MODEL)
    trig_a = _rope_trig(positions, A_HEAD_DIM // 2)
    trig_b = _rope_trig(positions, B_ROPE // 2)
    trig_c = _rope_trig(positions, C_KDIM // 2)
    for i in range(depth):
        m, j = i % 4, i // 4
        h = _ffn(h, norm_gains[i, 0], ffn_w_in[i, 0], ffn_w_out[i, 0])
        g = norm_gains[i, 1]
        if m == 0:
            h = _mixer_a(h, g, batch, seq_len, trig_a, a_w_in[j], a_q_gain[j], a_k_gain[j], a_w_out[j])
        elif m == 1:
            h = _mixer_b(h, g, batch, seq_len, trig_b, b_w_in[j], b_q_a_gain[j], b_w_q_up[j], b_kv_a_gain[j],
                         b_w_kv_up[j], b_q_gain[j], b_k_gain[j], b_w_out[j])
        elif m == 2:
            h = _mixer_c(h, g, batch, seq_len, trig_c, c_w_in[j], c_w_out[j])
        else:
            h = _mixer_d(h, g, batch, seq_len, d_w_in[j], d_conv_w[j], d_conv_b[j], d_w_rg[j], d_b_rg[j],
                         d_w_ig[j], d_b_ig[j], d_lru_param[j], d_w_out[j])
        h = _ffn(h, norm_gains[i, 2], ffn_w_in[i, 1], ffn_w_out[i, 1])
    return h.reshape(batch, seq_len, D_MODEL)
```

```python
import functools
import math

import jax
import jax.numpy as jnp
from jax import lax
from jax.experimental import pallas as pl
from jax.experimental.pallas import tpu as pltpu

F32 = jnp.float32
BF16 = jnp.bfloat16

D_MODEL = 1024
D_FF = 2816
NORM_EPS = 1e-6
GN_EPS = 1e-5
ROPE_THETA = 10000.0
NEG_INF = -1e30
LOG2E = math.log2(math.e)
LN2 = math.log(2.0)

A_HEADS = 16
A_HEAD_DIM = 64
A_PATTERNS = ((128, 1), (512, 4), (2048, 16))
A_WINDOW_STEPS = 128
A_GROUP_WIDTH = A_HEADS * A_HEAD_DIM

B_HEADS = 16
B_NOPE = 64
B_ROPE = 32
B_QK = B_NOPE + B_ROPE
B_VDIM = 64
B_Q_RANK = 384
B_KV_RANK = 256

C_HEADS = 8
C_KDIM = 128
C_VDIM = 256
C_CHUNK = 256

D_WIDTH = 1024
D_BLOCKS = 4
D_BLOCK = D_WIDTH // D_BLOCKS
D_CONV = 4
LRU_C = 8.0

LANES = 128
SUBLANES = 8
VMEM_LIMIT_BYTES = 52 * 1024 * 1024

NT_DIMS = (((1,), (1,)), ((), ()))
TN_DIMS = (((0,), (0,)), ((), ()))


def _params(*semantics):
    return pltpu.CompilerParams(dimension_semantics=semantics, vmem_limit_bytes=VMEM_LIMIT_BYTES)


def _rms(x, g):
    ms = jnp.mean(x * x, axis=-1, keepdims=True)
    return x * lax.rsqrt(ms + NORM_EPS) * g


def _expm1(x):
    u = jnp.exp(x)
    near = u == 1.0
    ratio = (u - 1.0) * x / jnp.where(near, 1.0, jnp.log(u))
    return jnp.where(x < -1.0, u - 1.0, jnp.where(near, x, ratio))


def _const_spec(shape):
    nd = len(shape)
    return pl.BlockSpec(shape, lambda *_: (0,) * nd, pipeline_mode=pl.Buffered(1))


def _trig_body(pos_ref, inv_ref, cos_ref, sin_ref):
    ang = pos_ref[...] * inv_ref[...]
    cos_ref[...] = jnp.cos(ang)
    sin_ref[...] = jnp.sin(ang)


def _rope_trig(positions, half):
    t = positions.size
    per_row = LANES // half
    rows = t // per_row
    pos_rep = jnp.repeat(positions.reshape(rows, per_row).astype(F32), half, axis=1)
    inv = ROPE_THETA ** (-jnp.arange(half, dtype=F32) * 2.0 / (2 * half))
    inv_row = jnp.tile(inv, per_row)[None, :]
    tr = min(rows, 2048)
    cos, sin = pl.pallas_call(
        _trig_body,
        grid=(rows // tr,),
        in_specs=[pl.BlockSpec((tr, LANES), lambda i: (i, 0)), pl.BlockSpec((1, LANES), lambda i: (0, 0))],
        out_specs=[pl.BlockSpec((tr, LANES), lambda i: (i, 0))] * 2,
        out_shape=[jax.ShapeDtypeStruct((rows, LANES), F32)] * 2,
        compiler_params=_params("parallel"),
        name="rope_trig",
    )(pos_rep, inv_row)
    return cos.reshape(t, half), sin.reshape(t, half)


def _rope_gain_rows(gain_row, shift_a, shift_b, scale):
    return jnp.stack([gain_row, jnp.roll(gain_row, shift_a), jnp.roll(gain_row, shift_b)]) * scale


FFN_TM = 512
FFN_TF = 256


def _ffn_body(x_ref, g_ref, wg_ref, wu_ref, wo_ref, o_ref, act_ref):
    x = x_ref[...]
    xn = _rms(x, g_ref[...]).astype(BF16)
    for j in range(D_FF // FFN_TF):
        sl = slice(j * FFN_TF, (j + 1) * FFN_TF)
        gate = jnp.dot(xn, wg_ref[:, sl], preferred_element_type=F32)
        up = jnp.dot(xn, wu_ref[:, sl], preferred_element_type=F32)
        act_ref[:, sl] = (gate * jax.nn.sigmoid(gate) * up).astype(BF16)
    y = jnp.dot(act_ref[...], wo_ref[...], preferred_element_type=F32)
    o_ref[...] = x + 0.5 * y


def _ffn(x, gain, w_in, w_out):
    t = x.shape[0]
    tm = min(FFN_TM, t)
    wg = w_in[:, :D_FF].astype(BF16)
    wu = w_in[:, D_FF:].astype(BF16)
    wo = w_out.astype(BF16)
    return pl.pallas_call(
        _ffn_body,
        grid=(t // tm,),
        in_specs=[
            pl.BlockSpec((tm, D_MODEL), lambda i: (i, 0)),
            _const_spec((1, D_MODEL)),
            _const_spec((D_MODEL, D_FF)),
            _const_spec((D_MODEL, D_FF)),
            _const_spec((D_FF, D_MODEL)),
        ],
        out_specs=pl.BlockSpec((tm, D_MODEL), lambda i: (i, 0)),
        out_shape=jax.ShapeDtypeStruct((t, D_MODEL), F32),
        scratch_shapes=[pltpu.VMEM((tm, D_FF), BF16)],
        compiler_params=_params("parallel"),
        name="ffn",
    )(x, gain[None, :], wg, wu, wo)


PROJ_TM = 512


def _proj_res_body(x_ref, a_ref, w_ref, o_ref):
    o_ref[...] = x_ref[...] + jnp.dot(a_ref[...], w_ref[...], preferred_element_type=F32)


def _proj_res(x, a, w):
    t, k = a.shape
    tm = min(PROJ_TM, t)
    row = lambda i: (i, 0)
    return pl.pallas_call(
        _proj_res_body,
        grid=(t // tm,),
        in_specs=[pl.BlockSpec((tm, D_MODEL), row), pl.BlockSpec((tm, k), row), _const_spec((k, D_MODEL))],
        out_specs=pl.BlockSpec((tm, D_MODEL), row),
        out_shape=jax.ShapeDtypeStruct((t, D_MODEL), F32),
        compiler_params=_params("parallel"),
        name="proj_res",
    )(x, a, w.astype(BF16))


A_TM = 512
A_TL = 512
A_QSCALE = A_HEAD_DIM ** -0.5 * LOG2E


A_STAGE_SLOTS = 8


def _a_proj_body(x_ref, g_ref, w_ref, gain_ref, seg_ref, c_ref, sa_ref, sb_ref, o0_ref, o1_ref, o2_ref,
                 xn_ref, stage_ref):
    j = pl.program_id(1)
    tm = x_ref.shape[0]
    out_refs = (o0_ref, o1_ref, o2_ref)

    @pl.when(j == 0)
    def _():
        xn_ref[...] = _rms(x_ref[...], g_ref[...]).astype(BF16)

    xn = xn_ref[...]

    def emit(val, col):
        g, within = divmod(col, A_GROUP_WIDTH)
        d = A_PATTERNS[g][1]
        if d == 1:
            out_refs[g][0, 0, :, within:within + LANES] = val.astype(BF16)
            return
        slot = (col // LANES) % A_STAGE_SLOTS
        stage_ref[slot] = val
        for r in range(d):
            rows = stage_ref[slot, pl.ds(r, tm // d, stride=d), :]
            out_refs[g][0, 0, :, r * A_GROUP_WIDTH + within:r * A_GROUP_WIDTH + within + LANES] = rows.astype(BF16)

    @pl.when(j < 2)
    def _():
        tab_c = c_ref[...] * gain_ref[0, 0:1, :]
        tab_a = sa_ref[...] * gain_ref[0, 1:2, :]
        tab_b = sb_ref[...] * gain_ref[0, 2:3, :]
        for c in range(w_ref.shape[1] // 256):
            y2 = jnp.dot(xn, w_ref[:, c * 256:(c + 1) * 256], preferred_element_type=F32)
            ss = jnp.dot((y2 * y2).astype(BF16), seg_ref[...], preferred_element_type=F32)
            inv = lax.rsqrt(ss + A_HEAD_DIM * NORM_EPS)
            for hh in range(2):
                y = y2[:, hh * LANES:(hh + 1) * LANES]
                rot = y * tab_c + pltpu.roll(y, 96, 1) * tab_a + pltpu.roll(y, 32, 1) * tab_b
                emit(rot * inv[:, hh * LANES:(hh + 1) * LANES], c * 256 + hh * LANES)

    @pl.when(j == 2)
    def _():
        for c in range(w_ref.shape[1] // 256):
            v2 = jnp.dot(xn, w_ref[:, c * 256:(c + 1) * 256], preferred_element_type=F32)
            for hh in range(2):
                emit(v2[:, hh * LANES:(hh + 1) * LANES], c * 256 + hh * LANES)


def _a_proj(x, gain, batch, seq_len, w_in, q_gain, k_gain, cos, sin_a, sin_b):
    t = x.shape[0]
    tm = min(A_TM, seq_len)
    tiles = seq_len // tm
    width = 3 * A_GROUP_WIDTH
    root = A_HEAD_DIM ** 0.5
    gains = jnp.stack([_rope_gain_rows(jnp.tile(q_gain, 2), 96, 32, root * A_QSCALE),
                       _rope_gain_rows(jnp.tile(k_gain, 2), 96, 32, root)])
    seg = jnp.kron(jnp.eye(256 // A_HEAD_DIM, dtype=F32), jnp.ones((A_HEAD_DIM, A_HEAD_DIM), F32)).astype(BF16)
    tab = pl.BlockSpec((tm, LANES), lambda i, j: (i, 0))
    out_map = lambda i, j: (j, i // tiles, i % tiles, 0)
    return pl.pallas_call(
        _a_proj_body,
        grid=(t // tm, 3),
        in_specs=[
            pl.BlockSpec((tm, D_MODEL), lambda i, j: (i, 0)),
            pl.BlockSpec((1, D_MODEL), lambda i, j: (0, 0)),
            pl.BlockSpec((D_MODEL, width), lambda i, j: (0, j)),
            pl.BlockSpec((1, 3, LANES), lambda i, j: (jnp.minimum(j, 1), 0, 0)),
            pl.BlockSpec((256, 256), lambda i, j: (0, 0)),
            tab, tab, tab,
        ],
        out_specs=[pl.BlockSpec((1, 1, tm // d, d * A_GROUP_WIDTH), out_map) for _, d in A_PATTERNS],
        out_shape=[jax.ShapeDtypeStruct((3, batch, seq_len // d, d * A_GROUP_WIDTH), BF16) for _, d in A_PATTERNS],
        scratch_shapes=[pltpu.VMEM((tm, D_MODEL), BF16), pltpu.VMEM((A_STAGE_SLOTS, tm, LANES), F32)],
        compiler_params=_params("parallel", "arbitrary"),
        name="a_proj",
    )(x, gain[None, :], w_in.astype(BF16), gains, seg, cos, sin_a, sin_b)


def _a_attn_body(q_ref, kc_ref, kp_ref, vc_ref, vp_ref, o_ref, m_ref, l_ref, *, whole_seq):
    li = pl.program_id(2)
    w = A_WINDOW_STEPS
    nblk = q_ref.shape[2] // w
    even = lax.broadcasted_iota(jnp.int32, (w, LANES), 1) < A_HEAD_DIM
    key = lax.broadcasted_iota(jnp.int32, (2 * w, 2 * w), 0)
    col = lax.broadcasted_iota(jnp.int32, (2 * w, 2 * w), 1)
    qry = jnp.where(col >= w, col - w, col)
    band_bias = jnp.where((key >= qry) & (key <= qry + w), 0.0, NEG_INF)
    if whole_seq:
        key1 = lax.broadcasted_iota(jnp.int32, (w, 2 * w), 0)
        col1 = lax.broadcasted_iota(jnp.int32, (w, 2 * w), 1)
        tri_bias = jnp.where(key1 <= jnp.where(col1 >= w, col1 - w, col1), 0.0, NEG_INF)
    else:
        is_first = jnp.where(li == 0, 1.0, 0.0)
        start_bias = band_bias + jnp.where(key < w, NEG_INF, 0.0) * is_first

    for rr, hp in [(rr, hp) for rr in range(o_ref.shape[2] // A_GROUP_WIDTH) for hp in range(A_HEADS // 2)]:
        ls = slice(rr * A_GROUP_WIDTH + hp * LANES, rr * A_GROUP_WIDTH + (hp + 1) * LANES)
        for j in range(nblk):
            rs = slice(j * w, (j + 1) * w)
            qb = q_ref[0, 0, rs, ls]
            zero = jnp.zeros_like(qb)
            q2 = jnp.concatenate([jnp.where(even, qb, zero), jnp.where(even, zero, qb)], axis=0)
            if j == 0 and whole_seq:
                kk, vv, bias = kc_ref[0, 0, rs, ls], vc_ref[0, 0, rs, ls], tri_bias
            elif j == 0:
                kk = jnp.concatenate([kp_ref[0, 0, :, ls], kc_ref[0, 0, rs, ls]], axis=0)
                vv = jnp.concatenate([vp_ref[0, 0, :, ls], vc_ref[0, 0, rs, ls]], axis=0)
                bias = start_bias
            else:
                kk = kc_ref[0, 0, (j - 1) * w:(j + 1) * w, ls]
                vv = vc_ref[0, 0, (j - 1) * w:(j + 1) * w, ls]
                bias = band_bias
            st = lax.dot_general(kk, q2, NT_DIMS, preferred_element_type=F32) + bias
            probs = []
            for hh in range(2):
                sh = st[:, hh * w:(hh + 1) * w]
                m = jnp.max(sh, axis=0, keepdims=True)
                ph = jnp.exp2(sh - m)
                m_ref[0, rr, hp, hh:hh + 1, rs] = m
                l_ref[0, rr, hp, hh:hh + 1, rs] = jnp.sum(ph, axis=0, keepdims=True)
                probs.append(ph.astype(BF16))
            p = jnp.concatenate(probs, axis=1)
            o2 = lax.dot_general(p, vv, TN_DIMS, preferred_element_type=F32)
            o_ref[0, rs, ls] = jnp.where(even, o2[:w], o2[w:]).astype(BF16)


def _a_attn(qkv, group, batch, seq_len):
    _, d = A_PATTERNS[group]
    sub_len = seq_len // d
    tl = min(A_TL, sub_len)
    w = A_WINDOW_STEPS
    rpb = A_TL // tl
    lanes = rpb * A_GROUP_WIDTH
    cur = lambda which: pl.BlockSpec((1, 1, tl, lanes), lambda b, r, li: (which, b, li, r))
    prev = lambda which: pl.BlockSpec(
        (1, 1, w, lanes), lambda b, r, li: (which, b, jnp.maximum(li * (tl // w) - 1, 0), r))
    stat = pl.BlockSpec((1, rpb, A_HEADS // 2, 2, tl), lambda b, r, li: (b, r, 0, 0, li))
    return pl.pallas_call(
        functools.partial(_a_attn_body, whole_seq=(tl == sub_len)),
        grid=(batch, d // rpb, sub_len // tl),
        in_specs=[cur(0), cur(1), prev(1), cur(2), prev(2)],
        out_specs=[pl.BlockSpec((1, tl, lanes), lambda b, r, li: (b, li, r)), stat, stat],
        out_shape=[
            jax.ShapeDtypeStruct((batch, sub_len, d * A_GROUP_WIDTH), BF16),
            jax.ShapeDtypeStruct((batch, d, A_HEADS // 2, 2, sub_len), F32),
            jax.ShapeDtypeStruct((batch, d, A_HEADS // 2, 2, sub_len), F32),
        ],
        compiler_params=_params("parallel", "parallel", "arbitrary"),
        name="a_attn",
    )(qkv, qkv, qkv, qkv, qkv)


def _a_out_body(x_ref, o0_ref, o1_ref, o2_ref, p1_ref, p2_ref, m_ref, l_ref, e_ref, w_ref, out_ref):
    def token_major(o_ref, perm_ref):
        d = o_ref.shape[2] // A_GROUP_WIDTH
        stacked = jnp.concatenate(
            [o_ref[0, :, r * A_GROUP_WIDTH:(r + 1) * A_GROUP_WIDTH] for r in range(d)], axis=0)
        return jnp.dot(perm_ref[...], stacked, preferred_element_type=F32)

    groups = (o0_ref[0].astype(F32), token_major(o1_ref, p1_ref), token_major(o2_ref, p2_ref))
    m0, m1, m2 = m_ref[0], m_ref[1], m_ref[2]
    top = jnp.maximum(jnp.maximum(m0, m1), m2)
    c0, c1, c2 = jnp.exp2(m0 - top), jnp.exp2(m1 - top), jnp.exp2(m2 - top)
    inv = 1.0 / (c0 * l_ref[0] + c1 * l_ref[1] + c2 * l_ref[2])
    merged = None
    for coef, o in zip((c0, c1, c2), groups):
        wt = coef * inv
        hi = wt.astype(BF16)
        lo = (wt - hi.astype(F32)).astype(BF16)
        wexp = (jnp.dot(hi, e_ref[...], preferred_element_type=F32)
                + jnp.dot(lo, e_ref[...], preferred_element_type=F32))
        merged = wexp * o if merged is None else merged + wexp * o
    out_ref[...] = x_ref[...] + jnp.dot(merged.astype(BF16), w_ref[...], preferred_element_type=F32)


def _a_out(x, outs, m3, l3, batch, seq_len, w_out):
    t = x.shape[0]
    tm = min(PROJ_TM, seq_len)
    tiles = seq_len // tm
    expand = jnp.repeat(jnp.eye(A_HEADS, dtype=BF16), A_HEAD_DIM, axis=1)
    perms = []
    for _, d in A_PATTERNS[1:]:
        src = jnp.arange(tm)
        dst = (src % (tm // d)) * d + src // (tm // d)
        perms.append(jnp.zeros((tm, tm), BF16).at[dst, src].set(1.0))
    row = lambda i: (i, 0)
    o_specs = [pl.BlockSpec((1, tm // d, d * A_GROUP_WIDTH), lambda i: (i // tiles, i % tiles, 0))
               for _, d in A_PATTERNS]
    s_spec = pl.BlockSpec((3, tm, A_HEADS), lambda i: (0, i, 0))
    return pl.pallas_call(
        _a_out_body,
        grid=(t // tm,),
        in_specs=[pl.BlockSpec((tm, D_MODEL), row), *o_specs, _const_spec((tm, tm)), _const_spec((tm, tm)),
                  s_spec, s_spec, _const_spec((A_HEADS, A_GROUP_WIDTH)), _const_spec((A_GROUP_WIDTH, D_MODEL))],
        out_specs=pl.BlockSpec((tm, D_MODEL), row),
        out_shape=jax.ShapeDtypeStruct((t, D_MODEL), F32),
        compiler_params=_params("parallel"),
        name="a_out",
    )(x, *outs, *perms, m3, l3, expand, w_out.astype(BF16))


def _mixer_a(x, gain, batch, seq_len, trig, w_in, q_gain, k_gain, w_out):
    t = x.shape[0]
    cos, sin = trig
    zero = jnp.zeros_like(sin)
    tab_c = jnp.tile(jnp.concatenate([cos, cos], axis=1), (1, 2))
    tab_sa = jnp.tile(jnp.concatenate([-sin, zero], axis=1), (1, 2))
    tab_sb = jnp.tile(jnp.concatenate([zero, sin], axis=1), (1, 2))
    qkv = _a_proj(x, gain, batch, seq_len, w_in, q_gain, k_gain, tab_c, tab_sa, tab_sb)
    outs, ms, ls = [], [], []
    for gi in range(len(A_PATTERNS)):
        o, m, l = _a_attn(qkv[gi], gi, batch, seq_len)
        outs.append(o)
        ms.append(m.transpose(0, 4, 1, 2, 3).reshape(t, A_HEADS))
        ls.append(l.transpose(0, 4, 1, 2, 3).reshape(t, A_HEADS))
    return _a_out(x, outs, jnp.stack(ms), jnp.stack(ls), batch, seq_len, w_out)


B_TM = 512
B_TQ = 256
B_TK = 256
B_HEAD_PAD = LANES
B_IN_PAD = 768
B_QSCALE = B_QK ** -0.5 * LOG2E


def _b_proj_body(x_ref, g_ref, win_ref, qag_ref, kvag_ref, wq_ref, wkv_ref, qg_ref, kg_ref, seg_ref,
                 c_ref, sa_ref, sb_ref, q_ref, k_ref, vt_ref):
    tm = x_ref.shape[0]
    xn = _rms(x_ref[...], g_ref[...]).astype(BF16)
    h = jnp.dot(xn, win_ref[...], preferred_element_type=F32)
    c_q = h[:, :B_Q_RANK]
    rest = h[:, B_Q_RANK:]
    cq = (_rms(c_q, qag_ref[...])).astype(BF16)
    lane_r = lax.broadcasted_iota(jnp.int32, rest.shape, 1)
    is_kv = lane_r < B_KV_RANK
    ms = jnp.sum(jnp.where(is_kv, rest * rest, 0.0), axis=-1, keepdims=True) * (1.0 / B_KV_RANK)
    ckv = jnp.where(is_kv, rest * lax.rsqrt(ms + NORM_EPS) * kvag_ref[...], rest).astype(BF16)
    cos, sin_a, sin_b = c_ref[...], sa_ref[...], sb_ref[...]
    q_tabs = (cos * qg_ref[0:1, :], sin_a * qg_ref[1:2, :], sin_b * qg_ref[2:3, :])
    k_tabs = (cos * kg_ref[0:1, :], sin_a * kg_ref[1:2, :], sin_b * kg_ref[2:3, :])

    def heads_norm_rope(y2, tabs, out_ref, lo):
        ss = jnp.dot((y2 * y2).astype(BF16), seg_ref[...], preferred_element_type=F32)
        inv = lax.rsqrt(ss + B_QK * NORM_EPS)
        for hh in range(2):
            y = y2[:, hh * LANES:(hh + 1) * LANES]
            rot = (y * tabs[0] + pltpu.roll(y, LANES - B_ROPE // 2, 1) * tabs[1]
                   + pltpu.roll(y, B_ROPE // 2, 1) * tabs[2])
            out_ref[:, lo + hh * LANES:lo + (hh + 1) * LANES] = (
                rot * inv[:, hh * LANES:(hh + 1) * LANES]).astype(BF16)

    for hd in range(B_HEADS // 2):
        sl = slice(hd * 256, (hd + 1) * 256)
        heads_norm_rope(jnp.dot(cq, wq_ref[:, sl], preferred_element_type=F32), q_tabs, q_ref, hd * 256)
        heads_norm_rope(jnp.dot(ckv, wkv_ref[:, sl], preferred_element_type=F32), k_tabs, k_ref, hd * 256)
    k_width = B_HEADS * B_HEAD_PAD
    for c in range(B_HEADS * B_VDIM // 256):
        sl = slice(k_width + c * 256, k_width + (c + 1) * 256)
        vc = jnp.dot(ckv, wkv_ref[:, sl], preferred_element_type=F32).astype(BF16)
        for s in range(tm // B_TK):
            vt_ref[0, s, c * 256:(c + 1) * 256, :] = vc[s * B_TK:(s + 1) * B_TK, :].T


def _b_proj(x, gain, batch, seq_len, w_in, q_a_gain, w_q_up, kv_a_gain, w_kv_up, q_gain, k_gain, cos, sin_a, sin_b):
    t = x.shape[0]
    tm = min(B_TM, seq_len)
    in_w = B_Q_RANK + B_KV_RANK + B_ROPE
    win = jnp.pad(w_in, ((0, 0), (0, B_IN_PAD - in_w))).astype(BF16)
    wq = jnp.pad(w_q_up.reshape(B_Q_RANK, B_HEADS, B_QK), ((0, 0), (0, 0), (0, B_HEAD_PAD - B_QK)))
    wq = wq.reshape(B_Q_RANK, B_HEADS * B_HEAD_PAD).astype(BF16)
    kv_in = B_IN_PAD - B_Q_RANK
    wkv = w_kv_up.reshape(B_KV_RANK, B_HEADS, B_NOPE + B_VDIM)
    wk = jnp.zeros((kv_in, B_HEADS, B_HEAD_PAD), F32)
    wk = wk.at[:B_KV_RANK, :, :B_NOPE].set(wkv[:, :, :B_NOPE])
    wk = wk.at[B_KV_RANK:B_KV_RANK + B_ROPE, :, B_NOPE:B_QK].set(
        jnp.broadcast_to(jnp.eye(B_ROPE, dtype=F32)[:, None, :], (B_ROPE, B_HEADS, B_ROPE)))
    wv = jnp.zeros((kv_in, B_HEADS, B_VDIM), F32).at[:B_KV_RANK].set(wkv[:, :, B_NOPE:])
    wkv_full = jnp.concatenate(
        [wk.reshape(kv_in, B_HEADS * B_HEAD_PAD), wv.reshape(kv_in, B_HEADS * B_VDIM)], axis=1).astype(BF16)
    root = B_QK ** 0.5
    shift_a, shift_b = LANES - B_ROPE // 2, B_ROPE // 2
    pad_gain = lambda gn: jnp.pad(gn, (0, B_HEAD_PAD - B_QK))
    q_rows = _rope_gain_rows(pad_gain(q_gain), shift_a, shift_b, root * B_QSCALE)
    k_rows = _rope_gain_rows(pad_gain(k_gain), shift_a, shift_b, root)
    kvag = jnp.pad(kv_a_gain, (0, kv_in - B_KV_RANK))[None, :]
    seg = jnp.kron(jnp.eye(2, dtype=F32), jnp.ones((B_HEAD_PAD, B_HEAD_PAD), F32)).astype(BF16)
    row = lambda i: (i, 0)
    tab = pl.BlockSpec((tm, LANES), row)
    tiles = seq_len // tm
    return pl.pallas_call(
        _b_proj_body,
        grid=(t // tm,),
        in_specs=[
            pl.BlockSpec((tm, D_MODEL), row),
            _const_spec((1, D_MODEL)),
            _const_spec((D_MODEL, B_IN_PAD)),
            _const_spec((1, B_Q_RANK)),
            _const_spec((1, kv_in)),
            _const_spec((B_Q_RANK, B_HEADS * B_HEAD_PAD)),
            _const_spec((kv_in, B_HEADS * (B_HEAD_PAD + B_VDIM))),
            _const_spec((3, B_HEAD_PAD)),
            _const_spec((3, B_HEAD_PAD)),
            _const_spec((2 * B_HEAD_PAD, 2 * B_HEAD_PAD)),
            tab, tab, tab,
        ],
        out_specs=[
            pl.BlockSpec((tm, B_HEADS * B_HEAD_PAD), row),
            pl.BlockSpec((tm, B_HEADS * B_HEAD_PAD), row),
            pl.BlockSpec((1, tm // B_TK, B_HEADS * B_VDIM, B_TK), lambda i: (i // tiles, i % tiles, 0, 0)),
        ],
        out_shape=[
            jax.ShapeDtypeStruct((t, B_HEADS * B_HEAD_PAD), BF16),
            jax.ShapeDtypeStruct((t, B_HEADS * B_HEAD_PAD), BF16),
            jax.ShapeDtypeStruct((batch, seq_len // B_TK, B_HEADS * B_VDIM, B_TK), BF16),
        ],
        compiler_params=_params("parallel"),
        name="b_proj",
    )(x, gain[None, :], win, q_a_gain[None, :], kvag, wq, wkv_full, q_rows, k_rows, seg, cos, sin_a, sin_b)


def _b_attn_body(q_ref, k_ref, vt_ref, o_ref, m_ref, l_ref, acc_ref, ot_ref):
    seq_len = q_ref.shape[1]
    tq, tk = B_TQ, B_TK
    key = lax.broadcasted_iota(jnp.int32, (tk, tq), 0)
    qry = lax.broadcasted_iota(jnp.int32, (tk, tq), 1)
    diag_bias = jnp.where(key <= qry, 0.0, NEG_INF)
    n = seq_len // tq
    for kj in range(n):
        lo = kj * tq
        for hh in range(2):
            hl = slice(hh * LANES, (hh + 1) * LANES)
            k = k_ref[0, lo:lo + tk, hl]
            vt = vt_ref[0, kj, hh * B_VDIM:(hh + 1) * B_VDIM, :]
            st = lax.dot_general(k, q_ref[0, lo:, hl], NT_DIMS, preferred_element_type=F32)
            diag = st[:, :tq] + diag_bias
            st = diag if kj == n - 1 else jnp.concatenate([diag, st[:, tq:]], axis=1)
            blk_max = jnp.max(st, axis=0, keepdims=True)
            if kj == 0:
                m_new = blk_max
                p = jnp.exp2(st - m_new)
                l = jnp.sum(p, axis=0, keepdims=True)
                acc = jnp.dot(vt, p.astype(BF16), preferred_element_type=F32)
            else:
                m = m_ref[hh, :, lo:]
                m_new = jnp.maximum(m, blk_max)
                alpha = jnp.exp2(m - m_new)
                p = jnp.exp2(st - m_new)
                l = alpha * l_ref[hh, :, lo:] + jnp.sum(p, axis=0, keepdims=True)
                acc = alpha * acc_ref[hh, :, lo:] + jnp.dot(vt, p.astype(BF16), preferred_element_type=F32)
            m_ref[hh, :, lo:] = m_new
            l_ref[hh, :, lo:] = l
            acc_ref[hh, :, lo:] = acc
            ot_ref[hh * B_VDIM:(hh + 1) * B_VDIM, :] = acc_ref[hh, :, lo:lo + tq] * (1.0 / l_ref[hh, :, lo:lo + tq])
        o_ref[0, lo:lo + tq, :] = ot_ref[...].T.astype(BF16)


def _b_attn(q, k, vt, batch, seq_len):
    return pl.pallas_call(
        _b_attn_body,
        grid=(batch, B_HEADS // 2),
        in_specs=[
            pl.BlockSpec((1, seq_len, 2 * B_HEAD_PAD), lambda b, hp: (b, 0, hp)),
            pl.BlockSpec((1, seq_len, 2 * B_HEAD_PAD), lambda b, hp: (b, 0, hp)),
            pl.BlockSpec((1, seq_len // B_TK, 2 * B_VDIM, B_TK), lambda b, hp: (b, 0, hp, 0)),
        ],
        out_specs=pl.BlockSpec((1, seq_len, 2 * B_VDIM), lambda b, hp: (b, 0, hp)),
        out_shape=jax.ShapeDtypeStruct((batch, seq_len, B_HEADS * B_VDIM), BF16),
        scratch_shapes=[
            pltpu.VMEM((2, 1, seq_len), F32),
            pltpu.VMEM((2, 1, seq_len), F32),
            pltpu.VMEM((2, B_VDIM, seq_len), F32),
            pltpu.VMEM((2 * B_VDIM, B_TQ), F32),
        ],
        compiler_params=_params("parallel", "parallel"),
        name="b_attn",
    )(q, k, vt)


def _mixer_b(x, gain, batch, seq_len, trig, w_in, q_a_gain, w_q_up, kv_a_gain, w_kv_up, q_gain, k_gain, w_out):
    t = x.shape[0]
    cos, sin = trig
    one = jnp.ones((t, B_NOPE), F32)
    zero64 = jnp.zeros((t, B_NOPE), F32)
    z16 = jnp.zeros_like(sin)
    tail = jnp.zeros((t, B_HEAD_PAD - B_QK), F32)
    tab_c = jnp.concatenate([one, cos, cos, tail + 1.0], axis=1)
    tab_sa = jnp.concatenate([zero64, -sin, z16, tail], axis=1)
    tab_sb = jnp.concatenate([zero64, z16, sin, tail], axis=1)
    q, k, vt = _b_proj(x, gain, batch, seq_len, w_in, q_a_gain, w_q_up, kv_a_gain, w_kv_up, q_gain, k_gain,
                       tab_c, tab_sa, tab_sb)
    shp = lambda a: a.reshape(batch, seq_len, a.shape[-1])
    o = _b_attn(shp(q), shp(k), vt, batch, seq_len)
    return _proj_res(x, o.reshape(t, B_HEADS * B_VDIM), w_out)


C_TM = 512
C_QK_W = C_HEADS * C_KDIM
C_V_W = C_HEADS * C_VDIM


def _c_proj_body(x_ref, g_ref, w_ref, c_ref, s_ref, o_ref, xn_ref):
    j = pl.program_id(1)
    width = w_ref.shape[1]

    @pl.when(j == 0)
    def _():
        xn_ref[...] = _rms(x_ref[...], g_ref[...]).astype(BF16)

    xn = xn_ref[...]

    @pl.when(j == 0)
    def _():
        cos, sin = c_ref[...], s_ref[...]
        for c in range(width // 256):
            y2 = jnp.dot(xn, w_ref[:, c * 256:(c + 1) * 256], preferred_element_type=F32)
            for hh in range(2):
                y = y2[:, hh * LANES:(hh + 1) * LANES]
                out = y * cos + pltpu.roll(y, C_KDIM // 2, 1) * sin
                lo = c * 256 + hh * LANES
                if lo >= C_QK_W:
                    out = out * (C_KDIM ** -0.5)
                o_ref[0, :, lo:lo + LANES] = out.astype(BF16)

    @pl.when(j == 1)
    def _():
        for c in range(width // 256):
            sl = slice(c * 256, (c + 1) * 256)
            o_ref[0, :, sl] = jnp.dot(xn, w_ref[:, sl], preferred_element_type=F32).astype(BF16)

    @pl.when(j == 2)
    def _():
        for c in range(width // 256):
            sl = slice(c * 256, (c + 1) * 256)
            gt = jnp.dot(xn, w_ref[:, sl], preferred_element_type=F32)
            o_ref[0, :, sl] = (gt * jax.nn.sigmoid(gt)).astype(BF16)


def _c_proj(x, gain, w_in, cos, sin):
    t = x.shape[0]
    tm = min(C_TM, t)
    width = 2 * C_QK_W
    tab = pl.BlockSpec((tm, LANES), lambda i, j: (i, 0))
    return pl.pallas_call(
        _c_proj_body,
        grid=(t // tm, 3),
        in_specs=[
            pl.BlockSpec((tm, D_MODEL), lambda i, j: (i, 0)),
            pl.BlockSpec((1, D_MODEL), lambda i, j: (0, 0)),
            pl.BlockSpec((D_MODEL, width), lambda i, j: (0, j)),
            tab, tab,
        ],
        out_specs=pl.BlockSpec((1, tm, width), lambda i, j: (j, i, 0)),
        out_shape=jax.ShapeDtypeStruct((3, t, width), BF16),
        scratch_shapes=[pltpu.VMEM((tm, D_MODEL), BF16)],
        compiler_params=_params("parallel", "arbitrary"),
        name="c_proj",
    )(x, gain[None, :], w_in.astype(BF16), cos, sin)


def _c_ret_body(q_ref, k_ref, v_ref, dec_ref, xi_ref, zeta_ref, cd_ref, y_ref, r_ref):
    n_chunks = q_ref.shape[2] // C_CHUNK
    r_ref[...] = jnp.zeros_like(r_ref)
    decay = dec_ref[0]
    xi = xi_ref[0]
    zeta = zeta_ref[0]
    cd = cd_ref[0]

    def step(n, carry):
        rows = pl.ds(pl.multiple_of(n * C_CHUNK, C_CHUNK), C_CHUNK)
        qc = q_ref[0, 0, rows, :]
        kc = k_ref[0, 0, rows, :]
        vc = v_ref[0, 0, rows, :]
        s = lax.dot_general(qc, kc, NT_DIMS, preferred_element_type=F32) * decay
        inner = jnp.dot(s.astype(BF16), vc, preferred_element_type=F32)
        r_old = r_ref[...]
        cross = jnp.dot(qc, r_old.astype(BF16), preferred_element_type=F32) * xi
        kz = (kc.astype(F32) * zeta).astype(BF16)
        r_ref[...] = cd * r_old + lax.dot_general(kz, vc, TN_DIMS, preferred_element_type=F32)
        y = inner + cross
        mu = jnp.mean(y, axis=-1, keepdims=True)
        yc = y - mu
        var = jnp.mean(yc * yc, axis=-1, keepdims=True)
        y_ref[0, rows, :] = (yc * lax.rsqrt(var + GN_EPS)).astype(BF16)
        return carry

    lax.fori_loop(0, n_chunks, step, 0)


def _c_ret(qkv, batch, seq_len):
    cc = C_CHUNK
    log_g = jnp.log(1.0 - 2.0 ** (-5.0 - jnp.arange(C_HEADS, dtype=F32)))
    idx = jnp.arange(cc, dtype=F32)
    diff = idx[:, None] - idx[None, :]
    causal = diff >= 0
    decay = jnp.where(causal[None], jnp.exp(jnp.where(causal, diff, 0.0)[None] * log_g[:, None, None]), 0.0)
    xi = jnp.exp((idx + 1.0)[None, :] * log_g[:, None])[:, :, None]
    zeta = jnp.exp((cc - 1.0 - idx)[None, :] * log_g[:, None])[:, :, None]
    cdec = jnp.broadcast_to(jnp.exp(cc * log_g)[:, None, None], (C_HEADS, 1, C_VDIM))
    per_head = lambda shape: pl.BlockSpec((1,) + shape, lambda b, h: (h, 0, 0))
    return pl.pallas_call(
        _c_ret_body,
        grid=(batch, C_HEADS),
        in_specs=[
            pl.BlockSpec((1, 1, seq_len, C_KDIM), lambda b, h: (0, b, 0, h)),
            pl.BlockSpec((1, 1, seq_len, C_KDIM), lambda b, h: (0, b, 0, C_HEADS + h)),
            pl.BlockSpec((1, 1, seq_len, C_VDIM), lambda b, h: (1, b, 0, h)),
            per_head((cc, cc)), per_head((cc, 1)), per_head((cc, 1)), per_head((1, C_VDIM)),
        ],
        out_specs=pl.BlockSpec((1, seq_len, C_VDIM), lambda b, h: (b, 0, h)),
        out_shape=jax.ShapeDtypeStruct((batch, seq_len, C_V_W), BF16),
        scratch_shapes=[pltpu.VMEM((C_KDIM, C_VDIM), F32)],
        compiler_params=_params("parallel", "parallel"),
        name="c_ret",
    )(qkv, qkv, qkv, decay, xi, zeta, cdec)


def _c_out_body(x_ref, y_ref, gate_ref, w_ref, o_ref):
    a = (y_ref[...].astype(F32) * gate_ref[0].astype(F32)).astype(BF16)
    o_ref[...] = x_ref[...] + jnp.dot(a, w_ref[...], preferred_element_type=F32)


def _c_out(x, y, proj, w_out):
    t = x.shape[0]
    tm = min(PROJ_TM, t)
    row = lambda i: (i, 0)
    return pl.pallas_call(
        _c_out_body,
        grid=(t // tm,),
        in_specs=[pl.BlockSpec((tm, D_MODEL), row), pl.BlockSpec((tm, C_V_W), row),
                  pl.BlockSpec((1, tm, C_V_W), lambda i: (2, i, 0)), _const_spec((C_V_W, D_MODEL))],
        out_specs=pl.BlockSpec((tm, D_MODEL), row),
        out_shape=jax.ShapeDtypeStruct((t, D_MODEL), F32),
        compiler_params=_params("parallel"),
        name="c_out",
    )(x, y, proj, w_out.astype(BF16))


def _mixer_c(x, gain, batch, seq_len, trig, w_in, w_out):
    t = x.shape[0]
    cos, sin = trig
    tab_c = jnp.concatenate([cos, cos], axis=1)
    tab_s = jnp.concatenate([-sin, sin], axis=1)
    proj = _c_proj(x, gain, w_in, tab_c, tab_s)
    y = _c_ret(proj.reshape(3, batch, seq_len, 2 * C_QK_W), batch, seq_len)
    return _c_out(x, y.reshape(t, C_V_W), proj, w_out)


D_TS = 256


def _d_body(x_ref, g_ref, win_ref, cw_ref, cb_ref, wrg_ref, brg_ref, wig_ref, big_ref, lru_ref,
            o_ref, ubuf_ref, a_ref, b_ref, hs_ref, h_ref):
    ti = pl.program_id(1)
    ts = x_ref.shape[1]
    pad = SUBLANES

    @pl.when(ti == 0)
    def _():
        ubuf_ref[0:pad, :] = jnp.zeros((pad, D_WIDTH), F32)
        h_ref[...] = jnp.zeros_like(h_ref)

    xn = _rms(x_ref[0], g_ref[...]).astype(BF16)
    gate = jnp.dot(xn, win_ref[:, :D_WIDTH], preferred_element_type=F32)
    u = jnp.dot(xn, win_ref[:, D_WIDTH:], preferred_element_type=F32)
    ubuf_ref[pad:pad + ts, :] = u
    uc = cb_ref[...] + cw_ref[D_CONV - 1:D_CONV, :] * u
    for k in range(D_CONV - 1):
        off = pad - (D_CONV - 1) + k
        uc = uc + cw_ref[k:k + 1, :] * ubuf_ref[off:off + ts, :]
    ubuf_ref[0:pad, :] = ubuf_ref[ts:ts + pad, :]

    ucb = uc.astype(BF16)
    softplus_neg = jax.nn.softplus(-lru_ref[...])
    for n in range(D_BLOCKS):
        sl = slice(n * D_BLOCK, (n + 1) * D_BLOCK)
        r = jax.nn.sigmoid(jnp.dot(ucb[:, sl], wrg_ref[n], preferred_element_type=F32) + brg_ref[:, sl])
        ig = jax.nn.sigmoid(jnp.dot(ucb[:, sl], wig_ref[n], preferred_element_type=F32) + big_ref[:, sl])
        log_a = -LRU_C * r * softplus_neg[:, sl]
        a_ref[:, sl] = jnp.exp(log_a)
        b_ref[:, sl] = jnp.sqrt(jnp.maximum(-_expm1(2.0 * log_a), 0.0)) * (ig * uc[:, sl])

    row = lax.broadcasted_iota(jnp.int32, (SUBLANES, D_WIDTH), 0)

    def scan_group(j, h):
        rows = pl.ds(pl.multiple_of(j * SUBLANES, SUBLANES), SUBLANES)
        a = a_ref[rows, :]
        b = b_ref[rows, :]
        for s in (1, 2, 4):
            keep = row >= s
            b = jnp.where(keep, a * pltpu.roll(b, s, 0) + b, b)
            a = jnp.where(keep, a * pltpu.roll(a, s, 0), a)
        hs = a * h + b
        hs_ref[rows, :] = hs
        return jnp.broadcast_to(hs[SUBLANES - 1:SUBLANES, :], (SUBLANES, D_WIDTH))

    h_ref[...] = lax.fori_loop(0, ts // SUBLANES, scan_group, h_ref[...])
    o_ref[0] = (jax.nn.gelu(gate) * hs_ref[...]).astype(BF16)


def _d_main(x3, gain, w_in, conv_w, conv_b, w_rg, b_rg, w_ig, b_ig, lru_param):
    batch, seq_len, _ = x3.shape
    ts = min(D_TS, seq_len)
    vec = lambda a: a[None, :]
    return pl.pallas_call(
        _d_body,
        grid=(batch, seq_len // ts),
        in_specs=[
            pl.BlockSpec((1, ts, D_MODEL), lambda b, i: (b, i, 0)),
            _const_spec((1, D_MODEL)),
            _const_spec((D_MODEL, 2 * D_WIDTH)),
            _const_spec((D_CONV, D_WIDTH)),
            _const_spec((1, D_WIDTH)),
            _const_spec((D_BLOCKS, D_BLOCK, D_BLOCK)),
            _const_spec((1, D_WIDTH)),
            _const_spec((D_BLOCKS, D_BLOCK, D_BLOCK)),
            _const_spec((1, D_WIDTH)),
            _const_spec((1, D_WIDTH)),
        ],
        out_specs=pl.BlockSpec((1, ts, D_WIDTH), lambda b, i: (b, i, 0)),
        out_shape=jax.ShapeDtypeStruct((batch, seq_len, D_WIDTH), BF16),
        scratch_shapes=[
            pltpu.VMEM((ts + SUBLANES, D_WIDTH), F32),
            pltpu.VMEM((ts, D_WIDTH), F32),
            pltpu.VMEM((ts, D_WIDTH), F32),
            pltpu.VMEM((ts, D_WIDTH), F32),
            pltpu.VMEM((SUBLANES, D_WIDTH), F32),
        ],
        compiler_params=_params("parallel", "arbitrary"),
        name="d_main",
    )(x3, vec(gain), w_in.astype(BF16), conv_w, vec(conv_b), w_rg.astype(BF16), vec(b_rg),
      w_ig.astype(BF16), vec(b_ig), vec(lru_param))


def _mixer_d(x, gain, batch, seq_len, w_in, conv_w, conv_b, w_rg, b_rg, w_ig, b_ig, lru_param, w_out):
    t = x.shape[0]
    y = _d_main(x.reshape(batch, seq_len, D_MODEL), gain, w_in, conv_w, conv_b, w_rg, b_rg, w_ig, b_ig, lru_param)
    return _proj_res(x, y.reshape(t, D_WIDTH), w_out)


def kernel(x, positions, norm_gains, ffn_w_in, ffn_w_out, a_w_in, a_q_gain, a_k_gain, a_w_out, b_w_in, b_q_a_gain, b_w_q_up, b_kv_a_gain, b_w_kv_up, b_q_gain, b_k_gain, b_w_out, c_w_in, c_w_out, d_w_in, d_conv_w, d_conv_b, d_w_rg, d_b_rg, d_w_ig, d_b_ig, d_lru_param, d_w_out):
    batch, seq_len, _ = x.shape
    depth = norm_gains.shape[0]
    t = batch * seq_len
    h = x.reshape(t, D_MODEL)
    trig_a = _rope_trig(positions, A_HEAD_DIM // 2)
    trig_b = _rope_trig(positions, B_ROPE // 2)
    trig_c = _rope_trig(positions, C_KDIM // 2)
    for i in range(depth):
        m, j = i % 4, i // 4
        h = _ffn(h, norm_gains[i, 0], ffn_w_in[i, 0], ffn_w_out[i, 0])
        g = norm_gains[i, 1]
        if m == 0:
            h = _mixer_a(h, g, batch, seq_len, trig_a, a_w_in[j], a_q_gain[j], a_k_gain[j], a_w_out[j])
        elif m == 1:
            h = _mixer_b(h, g, batch, seq_len, trig_b, b_w_in[j], b_q_a_gain[j], b_w_q_up[j], b_kv_a_gain[j],
                         b_w_kv_up[j], b_q_gain[j], b_k_gain[j], b_w_out[j])
        elif m == 2:
            h = _mixer_c(h, g, batch, seq_len, trig_c, c_w_in[j], c_w_out[j])
        else:
            h = _mixer_d(h, g, batch, seq_len, d_w_in[j], d_conv_w[j], d_conv_b[j], d_w_rg[j], d_b_rg[j],
                         d_w_ig[j], d_b_ig[j], d_lru_param[j], d_w_out[j])
        h = _ffn(h, norm_gains[i, 2], ffn_w_in[i, 1], ffn_w_out[i, 1])
    return h.reshape(batch, seq_len, D_MODEL)
```

```python
import functools
import math

import jax
import jax.numpy as jnp
from jax import lax
from jax.experimental import pallas as pl
from jax.experimental.pallas import tpu as pltpu

F32 = jnp.float32
BF16 = jnp.bfloat16

D_MODEL = 1024
D_FF = 2816
NORM_EPS = 1e-6
GN_EPS = 1e-5
ROPE_THETA = 10000.0
NEG_INF = -1e30
LOG2E = math.log2(math.e)

A_HEADS = 16
A_HEAD_DIM = 64
A_PATTERNS = ((128, 1), (512, 4), (2048, 16))
A_WINDOW_STEPS = 128
A_GROUP_WIDTH = A_HEADS * A_HEAD_DIM

B_HEADS = 16
B_NOPE = 64
B_ROPE = 32
B_QK = B_NOPE + B_ROPE
B_VDIM = 64
B_Q_RANK = 384
B_KV_RANK = 256

C_HEADS = 8
C_KDIM = 128
C_VDIM = 256
C_CHUNK = 256

D_WIDTH = 1024
D_BLOCKS = 4
D_BLOCK = D_WIDTH // D_BLOCKS
D_CONV = 4
LRU_C = 8.0

LANES = 128
SUBLANES = 8
VMEM_LIMIT_BYTES = 52 * 1024 * 1024

NT_DIMS = (((1,), (1,)), ((), ()))
TN_DIMS = (((0,), (0,)), ((), ()))


def _params(*semantics):
    return pltpu.CompilerParams(dimension_semantics=semantics, vmem_limit_bytes=VMEM_LIMIT_BYTES)


def _rms(x, g):
    ms = jnp.mean(x * x, axis=-1, keepdims=True)
    return x * lax.rsqrt(ms + NORM_EPS) * g


def _const_spec(shape):
    nd = len(shape)
    return pl.BlockSpec(shape, lambda *_: (0,) * nd, pipeline_mode=pl.Buffered(1))


def _trig_body(pos_ref, inv_ref, cos_ref, sin_ref):
    ang = pos_ref[...] * inv_ref[...]
    cos_ref[...] = jnp.cos(ang)
    sin_ref[...] = jnp.sin(ang)


def _rope_trig(positions, half):
    t = positions.size
    per_row = LANES // half
    rows = t // per_row
    pos_rep = jnp.repeat(positions.reshape(rows, per_row).astype(F32), half, axis=1)
    inv = ROPE_THETA ** (-jnp.arange(half, dtype=F32) * 2.0 / (2 * half))
    inv_row = jnp.tile(inv, per_row)[None, :]
    tr = min(rows, 2048)
    cos, sin = pl.pallas_call(
        _trig_body,
        grid=(rows // tr,),
        in_specs=[pl.BlockSpec((tr, LANES), lambda i: (i, 0)), pl.BlockSpec((1, LANES), lambda i: (0, 0))],
        out_specs=[pl.BlockSpec((tr, LANES), lambda i: (i, 0))] * 2,
        out_shape=[jax.ShapeDtypeStruct((rows, LANES), F32)] * 2,
        compiler_params=_params("parallel"),
        name="rope_trig",
    )(pos_rep, inv_row)
    return cos.reshape(t, half), sin.reshape(t, half)


def _rope_gain_rows(gain_row, shift_a, shift_b, scale):
    return jnp.stack([gain_row, jnp.roll(gain_row, shift_a), jnp.roll(gain_row, shift_b)]) * scale


FFN_TM = 512
FFN_TF = 256


def _ffn_body(*refs, pre):
    if pre == "proj":
        a_ref, wp_ref, *refs = refs
        delta = jnp.dot(a_ref[...], wp_ref[...], preferred_element_type=F32)
    elif pre == "gated":
        a_ref, gate_ref, wp_ref, *refs = refs
        a = (a_ref[...].astype(F32) * gate_ref[0].astype(F32)).astype(BF16)
        delta = jnp.dot(a, wp_ref[...], preferred_element_type=F32)
    x_ref, g_ref, wg_ref, wu_ref, wo_ref, o_ref, act_ref = refs
    x = x_ref[...] if pre == "none" else x_ref[...] + delta
    xn = _rms(x, g_ref[...]).astype(BF16)
    for j in range(D_FF // FFN_TF):
        sl = slice(j * FFN_TF, (j + 1) * FFN_TF)
        gate = jnp.dot(xn, wg_ref[:, sl], preferred_element_type=F32)
        up = jnp.dot(xn, wu_ref[:, sl], preferred_element_type=F32)
        act_ref[:, sl] = (gate * jax.nn.sigmoid(gate) * up).astype(BF16)
    y = jnp.dot(act_ref[...], wo_ref[...], preferred_element_type=F32)
    o_ref[...] = x + 0.5 * y


def _ffn(x, gain, w_in, w_out, pre=None):
    t = x.shape[0]
    tm = min(FFN_TM, t)
    wg = w_in[:, :D_FF].astype(BF16)
    wu = w_in[:, D_FF:].astype(BF16)
    wo = w_out.astype(BF16)
    row = lambda i: (i, 0)
    if pre is None:
        kind, pre_args, pre_specs = "none", [], []
    elif len(pre) == 2:
        a, wp = pre
        kind, pre_args = "proj", [a, wp.astype(BF16)]
        pre_specs = [pl.BlockSpec((tm, a.shape[1]), row), _const_spec(wp.shape)]
    else:
        a, gate3, wp = pre
        kind, pre_args = "gated", [a, gate3, wp.astype(BF16)]
        pre_specs = [pl.BlockSpec((tm, a.shape[1]), row),
                     pl.BlockSpec((1, tm, a.shape[1]), lambda i: (2, i, 0)), _const_spec(wp.shape)]
    return pl.pallas_call(
        functools.partial(_ffn_body, pre=kind),
        grid=(t // tm,),
        in_specs=pre_specs + [
            pl.BlockSpec((tm, D_MODEL), row),
            _const_spec((1, D_MODEL)),
            _const_spec((D_MODEL, D_FF)),
            _const_spec((D_MODEL, D_FF)),
            _const_spec((D_FF, D_MODEL)),
        ],
        out_specs=pl.BlockSpec((tm, D_MODEL), row),
        out_shape=jax.ShapeDtypeStruct((t, D_MODEL), F32),
        scratch_shapes=[pltpu.VMEM((tm, D_FF), BF16)],
        compiler_params=_params("parallel"),
        name="ffn",
    )(*pre_args, x, gain[None, :], wg, wu, wo)


A_TM = 512
A_TL = 512
A_OUT_TM = 512
A_QSCALE = A_HEAD_DIM ** -0.5 * LOG2E
A_STAGE_SLOTS = 8


def _a_proj_body(x_ref, g_ref, w_ref, gain_ref, seg_ref, c_ref, sa_ref, sb_ref, o0_ref, o1_ref, o2_ref,
                 xn_ref, stage_ref, ybuf_ref):
    j = pl.program_id(1)
    tm = x_ref.shape[0]
    out_refs = (o0_ref, o1_ref, o2_ref)

    @pl.when(j == 0)
    def _():
        xn_ref[...] = _rms(x_ref[...], g_ref[...]).astype(BF16)

    xn = xn_ref[...]

    def emit(val, col):
        g, within = divmod(col, A_GROUP_WIDTH)
        d = A_PATTERNS[g][1]
        if d == 1:
            out_refs[g][0, 0, :, within:within + LANES] = val.astype(BF16)
            return
        slot = (col // LANES) % A_STAGE_SLOTS
        stage_ref[slot] = val
        for r in range(d):
            rows = stage_ref[slot, pl.ds(r, tm // d, stride=d), :]
            out_refs[g][0, 0, :, r * A_GROUP_WIDTH + within:r * A_GROUP_WIDTH + within + LANES] = rows.astype(BF16)

    @pl.when(j < 2)
    def _():
        tab_c = c_ref[...] * gain_ref[0, 0:1, :]
        tab_a = sa_ref[...] * gain_ref[0, 1:2, :]
        tab_b = sb_ref[...] * gain_ref[0, 2:3, :]
        for c in range(w_ref.shape[1] // 256):
            ybuf_ref[:, c * 256:(c + 1) * 256] = jnp.dot(
                xn, w_ref[:, c * 256:(c + 1) * 256], preferred_element_type=F32)
        for c in range(w_ref.shape[1] // 256):
            y2 = ybuf_ref[:, c * 256:(c + 1) * 256]
            ss = jnp.dot((y2 * y2).astype(BF16), seg_ref[...], preferred_element_type=F32)
            inv = lax.rsqrt(ss + A_HEAD_DIM * NORM_EPS)
            for hh in range(2):
                y = y2[:, hh * LANES:(hh + 1) * LANES]
                rot = y * tab_c + pltpu.roll(y, 96, 1) * tab_a + pltpu.roll(y, 32, 1) * tab_b
                emit(rot * inv[:, hh * LANES:(hh + 1) * LANES], c * 256 + hh * LANES)

    @pl.when(j == 2)
    def _():
        for c in range(w_ref.shape[1] // 256):
            v2 = jnp.dot(xn, w_ref[:, c * 256:(c + 1) * 256], preferred_element_type=F32)
            for hh in range(2):
                emit(v2[:, hh * LANES:(hh + 1) * LANES], c * 256 + hh * LANES)


def _a_proj(x, gain, batch, seq_len, w_in, q_gain, k_gain, cos, sin_a, sin_b):
    t = x.shape[0]
    tm = min(A_TM, seq_len)
    tiles = seq_len // tm
    width = 3 * A_GROUP_WIDTH
    root = A_HEAD_DIM ** 0.5
    gains = jnp.stack([_rope_gain_rows(jnp.tile(q_gain, 2), 96, 32, root * A_QSCALE),
                       _rope_gain_rows(jnp.tile(k_gain, 2), 96, 32, root)])
    seg = jnp.kron(jnp.eye(256 // A_HEAD_DIM, dtype=F32), jnp.ones((A_HEAD_DIM, A_HEAD_DIM), F32)).astype(BF16)
    tab = pl.BlockSpec((tm, LANES), lambda i, j: (i, 0))
    out_map = lambda i, j: (j, i // tiles, i % tiles, 0)
    return pl.pallas_call(
        _a_proj_body,
        grid=(t // tm, 3),
        in_specs=[
            pl.BlockSpec((tm, D_MODEL), lambda i, j: (i, 0)),
            pl.BlockSpec((1, D_MODEL), lambda i, j: (0, 0)),
            pl.BlockSpec((D_MODEL, width), lambda i, j: (0, j)),
            pl.BlockSpec((1, 3, LANES), lambda i, j: (jnp.minimum(j, 1), 0, 0)),
            pl.BlockSpec((256, 256), lambda i, j: (0, 0)),
            tab, tab, tab,
        ],
        out_specs=[pl.BlockSpec((1, 1, tm // d, d * A_GROUP_WIDTH), out_map) for _, d in A_PATTERNS],
        out_shape=[jax.ShapeDtypeStruct((3, batch, seq_len // d, d * A_GROUP_WIDTH), BF16) for _, d in A_PATTERNS],
        scratch_shapes=[pltpu.VMEM((tm, D_MODEL), BF16), pltpu.VMEM((A_STAGE_SLOTS, tm, LANES), F32),
                        pltpu.VMEM((tm, width), F32)],
        compiler_params=_params("parallel", "arbitrary"),
        name="a_proj",
    )(x, gain[None, :], w_in.astype(BF16), gains, seg, cos, sin_a, sin_b)


def _a_attn_body(q_ref, kc_ref, kp_ref, vc_ref, vp_ref, o_ref, m_ref, l_ref, *, whole_seq):
    li = pl.program_id(2)
    w = A_WINDOW_STEPS
    nblk = q_ref.shape[2] // w
    even = lax.broadcasted_iota(jnp.int32, (w, LANES), 1) < A_HEAD_DIM
    key = lax.broadcasted_iota(jnp.int32, (2 * w, 2 * w), 0)
    col = lax.broadcasted_iota(jnp.int32, (2 * w, 2 * w), 1)
    qry = jnp.where(col >= w, col - w, col)
    band_bias = jnp.where((key >= qry) & (key <= qry + w), 0.0, NEG_INF)
    if whole_seq:
        key1 = lax.broadcasted_iota(jnp.int32, (w, 2 * w), 0)
        col1 = lax.broadcasted_iota(jnp.int32, (w, 2 * w), 1)
        tri_bias = jnp.where(key1 <= jnp.where(col1 >= w, col1 - w, col1), 0.0, NEG_INF)
    else:
        is_first = jnp.where(li == 0, 1.0, 0.0)
        start_bias = band_bias + jnp.where(key < w, NEG_INF, 0.0) * is_first

    for rr, hp in [(rr, hp) for rr in range(o_ref.shape[2] // A_GROUP_WIDTH) for hp in range(A_HEADS // 2)]:
        ls = slice(rr * A_GROUP_WIDTH + hp * LANES, rr * A_GROUP_WIDTH + (hp + 1) * LANES)
        for j in range(nblk):
            rs = slice(j * w, (j + 1) * w)
            qb = q_ref[0, 0, rs, ls]
            zero = jnp.zeros_like(qb)
            q2 = jnp.concatenate([jnp.where(even, qb, zero), jnp.where(even, zero, qb)], axis=0)
            if j == 0 and whole_seq:
                kk, vv, bias = kc_ref[0, 0, rs, ls], vc_ref[0, 0, rs, ls], tri_bias
            elif j == 0:
                kk = jnp.concatenate([kp_ref[0, 0, :, ls], kc_ref[0, 0, rs, ls]], axis=0)
                vv = jnp.concatenate([vp_ref[0, 0, :, ls], vc_ref[0, 0, rs, ls]], axis=0)
                bias = start_bias
            else:
                kk = kc_ref[0, 0, (j - 1) * w:(j + 1) * w, ls]
                vv = vc_ref[0, 0, (j - 1) * w:(j + 1) * w, ls]
                bias = band_bias
            st = lax.dot_general(kk, q2, NT_DIMS, preferred_element_type=F32) + bias
            probs = []
            for hh in range(2):
                sh = st[:, hh * w:(hh + 1) * w]
                m = jnp.max(sh, axis=0, keepdims=True)
                ph = jnp.exp2(sh - m)
                m_ref[0, rr, hp, hh:hh + 1, rs] = m
                l_ref[0, rr, hp, hh:hh + 1, rs] = jnp.sum(ph, axis=0, keepdims=True)
                probs.append(ph.astype(BF16))
            p = jnp.concatenate(probs, axis=1)
            o2 = lax.dot_general(p, vv, TN_DIMS, preferred_element_type=F32)
            o_ref[0, rs, ls] = jnp.where(even, o2[:w], o2[w:]).astype(BF16)


def _a_attn(qkv, group, batch, seq_len):
    _, d = A_PATTERNS[group]
    sub_len = seq_len // d
    tl = min(A_TL, sub_len)
    w = A_WINDOW_STEPS
    rpb = A_TL // tl
    lanes = rpb * A_GROUP_WIDTH
    cur = lambda which: pl.BlockSpec((1, 1, tl, lanes), lambda b, r, li: (which, b, li, r))
    prev = lambda which: pl.BlockSpec(
        (1, 1, w, lanes), lambda b, r, li: (which, b, jnp.maximum(li * (tl // w) - 1, 0), r))
    stat = pl.BlockSpec((1, rpb, A_HEADS // 2, 2, tl), lambda b, r, li: (b, r, 0, 0, li))
    return pl.pallas_call(
        functools.partial(_a_attn_body, whole_seq=(tl == sub_len)),
        grid=(batch, d // rpb, sub_len // tl),
        in_specs=[cur(0), cur(1), prev(1), cur(2), prev(2)],
        out_specs=[pl.BlockSpec((1, tl, lanes), lambda b, r, li: (b, li, r)), stat, stat],
        out_shape=[
            jax.ShapeDtypeStruct((batch, sub_len, d * A_GROUP_WIDTH), BF16),
            jax.ShapeDtypeStruct((batch, d, A_HEADS // 2, 2, sub_len), F32),
            jax.ShapeDtypeStruct((batch, d, A_HEADS // 2, 2, sub_len), F32),
        ],
        compiler_params=_params("parallel", "parallel", "arbitrary"),
        name="a_attn",
    )(qkv, qkv, qkv, qkv, qkv)


def _a_out_body(x_ref, o0_ref, o1_ref, o2_ref, p1_ref, p2_ref, m_ref, l_ref, e_ref, w_ref, out_ref):
    def token_major(o_ref, perm_ref):
        d = o_ref.shape[2] // A_GROUP_WIDTH
        stacked = jnp.concatenate(
            [o_ref[0, :, r * A_GROUP_WIDTH:(r + 1) * A_GROUP_WIDTH] for r in range(d)], axis=0)
        return jnp.dot(perm_ref[...], stacked, preferred_element_type=F32)

    groups = (o0_ref[0].astype(F32), token_major(o1_ref, p1_ref), token_major(o2_ref, p2_ref))
    m0, m1, m2 = m_ref[0], m_ref[1], m_ref[2]
    top = jnp.maximum(jnp.maximum(m0, m1), m2)
    c0, c1, c2 = jnp.exp2(m0 - top), jnp.exp2(m1 - top), jnp.exp2(m2 - top)
    inv = 1.0 / (c0 * l_ref[0] + c1 * l_ref[1] + c2 * l_ref[2])
    merged = None
    for coef, o in zip((c0, c1, c2), groups):
        wt = coef * inv
        wexp = jnp.dot(wt.astype(BF16), e_ref[...], preferred_element_type=F32)
        merged = wexp * o if merged is None else merged + wexp * o
    out_ref[...] = x_ref[...] + jnp.dot(merged.astype(BF16), w_ref[...], preferred_element_type=F32)


def _a_out(x, outs, m3, l3, batch, seq_len, w_out):
    t = x.shape[0]
    tm = min(A_OUT_TM, seq_len)
    tiles = seq_len // tm
    expand = jnp.repeat(jnp.eye(A_HEADS, dtype=BF16), A_HEAD_DIM, axis=1)
    perms = []
    for _, d in A_PATTERNS[1:]:
        src = jnp.arange(tm)
        dst = (src % (tm // d)) * d + src // (tm // d)
        perms.append(jnp.zeros((tm, tm), BF16).at[dst, src].set(1.0))
    row = lambda i: (i, 0)
    o_specs = [pl.BlockSpec((1, tm // d, d * A_GROUP_WIDTH), lambda i: (i // tiles, i % tiles, 0))
               for _, d in A_PATTERNS]
    s_spec = pl.BlockSpec((3, tm, A_HEADS), lambda i: (0, i, 0))
    return pl.pallas_call(
        _a_out_body,
        grid=(t // tm,),
        in_specs=[pl.BlockSpec((tm, D_MODEL), row), *o_specs, _const_spec((tm, tm)), _const_spec((tm, tm)),
                  s_spec, s_spec, _const_spec((A_HEADS, A_GROUP_WIDTH)), _const_spec((A_GROUP_WIDTH, D_MODEL))],
        out_specs=pl.BlockSpec((tm, D_MODEL), row),
        out_shape=jax.ShapeDtypeStruct((t, D_MODEL), F32),
        compiler_params=_params("parallel"),
        name="a_out",
    )(x, *outs, *perms, m3, l3, expand, w_out.astype(BF16))


def _mixer_a(x, gain, batch, seq_len, trig, w_in, q_gain, k_gain, w_out):
    t = x.shape[0]
    cos, sin = trig
    zero = jnp.zeros_like(sin)
    tab_c = jnp.tile(jnp.concatenate([cos, cos], axis=1), (1, 2))
    tab_sa = jnp.tile(jnp.concatenate([-sin, zero], axis=1), (1, 2))
    tab_sb = jnp.tile(jnp.concatenate([zero, sin], axis=1), (1, 2))
    qkv = _a_proj(x, gain, batch, seq_len, w_in, q_gain, k_gain, tab_c, tab_sa, tab_sb)
    outs, ms, ls = [], [], []
    for gi in range(len(A_PATTERNS)):
        o, m, l = _a_attn(qkv[gi], gi, batch, seq_len)
        outs.append(o)
        ms.append(m.transpose(0, 4, 1, 2, 3).reshape(t, A_HEADS))
        ls.append(l.transpose(0, 4, 1, 2, 3).reshape(t, A_HEADS))
    return _a_out(x, outs, jnp.stack(ms), jnp.stack(ls), batch, seq_len, w_out)


B_TM = 512
B_TQ = 256
B_TK = 256
B_HEAD_PAD = LANES
B_IN_PAD = 768
B_QSCALE = B_QK ** -0.5 * LOG2E


def _b_proj_body(x_ref, g_ref, win_ref, qag_ref, kvag_ref, wq_ref, wkv_ref, qg_ref, kg_ref, seg_ref,
                 c_ref, sa_ref, sb_ref, q_ref, k_ref, vt_ref):
    tm = x_ref.shape[0]
    xn = _rms(x_ref[...], g_ref[...]).astype(BF16)
    h = jnp.dot(xn, win_ref[...], preferred_element_type=F32)
    c_q = h[:, :B_Q_RANK]
    rest = h[:, B_Q_RANK:]
    cq = (_rms(c_q, qag_ref[...])).astype(BF16)
    lane_r = lax.broadcasted_iota(jnp.int32, rest.shape, 1)
    is_kv = lane_r < B_KV_RANK
    ms = jnp.sum(jnp.where(is_kv, rest * rest, 0.0), axis=-1, keepdims=True) * (1.0 / B_KV_RANK)
    ckv = jnp.where(is_kv, rest * lax.rsqrt(ms + NORM_EPS) * kvag_ref[...], rest).astype(BF16)
    cos, sin_a, sin_b = c_ref[...], sa_ref[...], sb_ref[...]
    q_tabs = (cos * qg_ref[0:1, :], sin_a * qg_ref[1:2, :], sin_b * qg_ref[2:3, :])
    k_tabs = (cos * kg_ref[0:1, :], sin_a * kg_ref[1:2, :], sin_b * kg_ref[2:3, :])

    def heads_norm_rope(y2, tabs, out_ref, lo):
        ss = jnp.dot((y2 * y2).astype(BF16), seg_ref[...], preferred_element_type=F32)
        inv = lax.rsqrt(ss + B_QK * NORM_EPS)
        for hh in range(2):
            y = y2[:, hh * LANES:(hh + 1) * LANES]
            rot = (y * tabs[0] + pltpu.roll(y, LANES - B_ROPE // 2, 1) * tabs[1]
                   + pltpu.roll(y, B_ROPE // 2, 1) * tabs[2])
            out_ref[:, lo + hh * LANES:lo + (hh + 1) * LANES] = (
                rot * inv[:, hh * LANES:(hh + 1) * LANES]).astype(BF16)

    for hd in range(B_HEADS // 2):
        sl = slice(hd * 256, (hd + 1) * 256)
        heads_norm_rope(jnp.dot(cq, wq_ref[:, sl], preferred_element_type=F32), q_tabs, q_ref, hd * 256)
        heads_norm_rope(jnp.dot(ckv, wkv_ref[:, sl], preferred_element_type=F32), k_tabs, k_ref, hd * 256)
    k_width = B_HEADS * B_HEAD_PAD
    for c in range(B_HEADS * B_VDIM // 256):
        sl = slice(k_width + c * 256, k_width + (c + 1) * 256)
        vc = jnp.dot(ckv, wkv_ref[:, sl], preferred_element_type=F32).astype(BF16)
        for s in range(tm // B_TK):
            vt_ref[0, s, c * 256:(c + 1) * 256, :] = vc[s * B_TK:(s + 1) * B_TK, :].T


def _b_proj(x, gain, batch, seq_len, w_in, q_a_gain, w_q_up, kv_a_gain, w_kv_up, q_gain, k_gain, cos, sin_a, sin_b):
    t = x.shape[0]
    tm = min(B_TM, seq_len)
    in_w = B_Q_RANK + B_KV_RANK + B_ROPE
    win = jnp.pad(w_in, ((0, 0), (0, B_IN_PAD - in_w))).astype(BF16)
    wq = jnp.pad(w_q_up.reshape(B_Q_RANK, B_HEADS, B_QK), ((0, 0), (0, 0), (0, B_HEAD_PAD - B_QK)))
    wq = wq.reshape(B_Q_RANK, B_HEADS * B_HEAD_PAD).astype(BF16)
    kv_in = B_IN_PAD - B_Q_RANK
    wkv = w_kv_up.reshape(B_KV_RANK, B_HEADS, B_NOPE + B_VDIM)
    wk = jnp.zeros((kv_in, B_HEADS, B_HEAD_PAD), F32)
    wk = wk.at[:B_KV_RANK, :, :B_NOPE].set(wkv[:, :, :B_NOPE])
    wk = wk.at[B_KV_RANK:B_KV_RANK + B_ROPE, :, B_NOPE:B_QK].set(
        jnp.broadcast_to(jnp.eye(B_ROPE, dtype=F32)[:, None, :], (B_ROPE, B_HEADS, B_ROPE)))
    wv = jnp.zeros((kv_in, B_HEADS, B_VDIM), F32).at[:B_KV_RANK].set(wkv[:, :, B_NOPE:])
    wkv_full = jnp.concatenate(
        [wk.reshape(kv_in, B_HEADS * B_HEAD_PAD), wv.reshape(kv_in, B_HEADS * B_VDIM)], axis=1).astype(BF16)
    root = B_QK ** 0.5
    shift_a, shift_b = LANES - B_ROPE // 2, B_ROPE // 2
    pad_gain = lambda gn: jnp.pad(gn, (0, B_HEAD_PAD - B_QK))
    q_rows = _rope_gain_rows(pad_gain(q_gain), shift_a, shift_b, root * B_QSCALE)
    k_rows = _rope_gain_rows(pad_gain(k_gain), shift_a, shift_b, root)
    kvag = jnp.pad(kv_a_gain, (0, kv_in - B_KV_RANK))[None, :]
    seg = jnp.kron(jnp.eye(2, dtype=F32), jnp.ones((B_HEAD_PAD, B_HEAD_PAD), F32)).astype(BF16)
    row = lambda i: (i, 0)
    tab = pl.BlockSpec((tm, LANES), row)
    tiles = seq_len // tm
    return pl.pallas_call(
        _b_proj_body,
        grid=(t // tm,),
        in_specs=[
            pl.BlockSpec((tm, D_MODEL), row),
            _const_spec((1, D_MODEL)),
            _const_spec((D_MODEL, B_IN_PAD)),
            _const_spec((1, B_Q_RANK)),
            _const_spec((1, kv_in)),
            _const_spec((B_Q_RANK, B_HEADS * B_HEAD_PAD)),
            _const_spec((kv_in, B_HEADS * (B_HEAD_PAD + B_VDIM))),
            _const_spec((3, B_HEAD_PAD)),
            _const_spec((3, B_HEAD_PAD)),
            _const_spec((2 * B_HEAD_PAD, 2 * B_HEAD_PAD)),
            tab, tab, tab,
        ],
        out_specs=[
            pl.BlockSpec((tm, B_HEADS * B_HEAD_PAD), row),
            pl.BlockSpec((tm, B_HEADS * B_HEAD_PAD), row),
            pl.BlockSpec((1, tm // B_TK, B_HEADS * B_VDIM, B_TK), lambda i: (i // tiles, i % tiles, 0, 0)),
        ],
        out_shape=[
            jax.ShapeDtypeStruct((t, B_HEADS * B_HEAD_PAD), BF16),
            jax.ShapeDtypeStruct((t, B_HEADS * B_HEAD_PAD), BF16),
            jax.ShapeDtypeStruct((batch, seq_len // B_TK, B_HEADS * B_VDIM, B_TK), BF16),
        ],
        compiler_params=_params("parallel"),
        name="b_proj",
    )(x, gain[None, :], win, q_a_gain[None, :], kvag, wq, wkv_full, q_rows, k_rows, seg, cos, sin_a, sin_b)


def _b_attn_body(q_ref, k_ref, vt_ref, o_ref, m_ref, l_ref, acc_ref, ot_ref):
    seq_len = q_ref.shape[1]
    tq, tk = B_TQ, B_TK
    key = lax.broadcasted_iota(jnp.int32, (tk, tq), 0)
    qry = lax.broadcasted_iota(jnp.int32, (tk, tq), 1)
    diag_bias = jnp.where(key <= qry, 0.0, NEG_INF)
    n = seq_len // tq
    for kj in range(n):
        lo = kj * tq
        for hh in range(2):
            hl = slice(hh * LANES, (hh + 1) * LANES)
            k = k_ref[0, lo:lo + tk, hl]
            vt = vt_ref[0, kj, hh * B_VDIM:(hh + 1) * B_VDIM, :]
            st = lax.dot_general(k, q_ref[0, lo:, hl], NT_DIMS, preferred_element_type=F32)
            diag = st[:, :tq] + diag_bias
            st = diag if kj == n - 1 else jnp.concatenate([diag, st[:, tq:]], axis=1)
            blk_max = jnp.max(st, axis=0, keepdims=True)
            if kj == 0:
                m_new = blk_max
                p = jnp.exp2(st - m_new)
                l = jnp.sum(p, axis=0, keepdims=True)
                acc = jnp.dot(vt, p.astype(BF16), preferred_element_type=F32)
            else:
                m = m_ref[hh, :, lo:]
                m_new = jnp.maximum(m, blk_max)
                alpha = jnp.exp2(m - m_new)
                p = jnp.exp2(st - m_new)
                l = alpha * l_ref[hh, :, lo:] + jnp.sum(p, axis=0, keepdims=True)
                acc = alpha * acc_ref[hh, :, lo:] + jnp.dot(vt, p.astype(BF16), preferred_element_type=F32)
            m_ref[hh, :, lo:] = m_new
            l_ref[hh, :, lo:] = l
            acc_ref[hh, :, lo:] = acc
            ot_ref[hh * B_VDIM:(hh + 1) * B_VDIM, :] = acc_ref[hh, :, lo:lo + tq] * (1.0 / l_ref[hh, :, lo:lo + tq])
        o_ref[0, lo:lo + tq, :] = ot_ref[...].T.astype(BF16)


def _b_attn(q, k, vt, batch, seq_len):
    return pl.pallas_call(
        _b_attn_body,
        grid=(batch, B_HEADS // 2),
        in_specs=[
            pl.BlockSpec((1, seq_len, 2 * B_HEAD_PAD), lambda b, hp: (b, 0, hp)),
            pl.BlockSpec((1, seq_len, 2 * B_HEAD_PAD), lambda b, hp: (b, 0, hp)),
            pl.BlockSpec((1, seq_len // B_TK, 2 * B_VDIM, B_TK), lambda b, hp: (b, 0, hp, 0)),
        ],
        out_specs=pl.BlockSpec((1, seq_len, 2 * B_VDIM), lambda b, hp: (b, 0, hp)),
        out_shape=jax.ShapeDtypeStruct((batch, seq_len, B_HEADS * B_VDIM), BF16),
        scratch_shapes=[
            pltpu.VMEM((2, 1, seq_len), F32),
            pltpu.VMEM((2, 1, seq_len), F32),
            pltpu.VMEM((2, B_VDIM, seq_len), F32),
            pltpu.VMEM((2 * B_VDIM, B_TQ), F32),
        ],
        compiler_params=_params("parallel", "parallel"),
        name="b_attn",
    )(q, k, vt)


def _mixer_b(x, gain, batch, seq_len, trig, w_in, q_a_gain, w_q_up, kv_a_gain, w_kv_up, q_gain, k_gain, w_out):
    t = x.shape[0]
    cos, sin = trig
    one = jnp.ones((t, B_NOPE), F32)
    zero64 = jnp.zeros((t, B_NOPE), F32)
    z16 = jnp.zeros_like(sin)
    tail = jnp.zeros((t, B_HEAD_PAD - B_QK), F32)
    tab_c = jnp.concatenate([one, cos, cos, tail + 1.0], axis=1)
    tab_sa = jnp.concatenate([zero64, -sin, z16, tail], axis=1)
    tab_sb = jnp.concatenate([zero64, z16, sin, tail], axis=1)
    q, k, vt = _b_proj(x, gain, batch, seq_len, w_in, q_a_gain, w_q_up, kv_a_gain, w_kv_up, q_gain, k_gain,
                       tab_c, tab_sa, tab_sb)
    shp = lambda a: a.reshape(batch, seq_len, a.shape[-1])
    o = _b_attn(shp(q), shp(k), vt, batch, seq_len)
    return o.reshape(t, B_HEADS * B_VDIM), w_out


C_TM = 512
C_QK_W = C_HEADS * C_KDIM
C_V_W = C_HEADS * C_VDIM
C_HPS = 4


def _c_proj_body(x_ref, g_ref, w_ref, c_ref, s_ref, o_ref, xn_ref):
    j = pl.program_id(1)
    width = w_ref.shape[1]

    @pl.when(j == 0)
    def _():
        xn_ref[...] = _rms(x_ref[...], g_ref[...]).astype(BF16)

    xn = xn_ref[...]

    @pl.when(j == 0)
    def _():
        cos, sin = c_ref[...], s_ref[...]
        for c in range(width // 256):
            y2 = jnp.dot(xn, w_ref[:, c * 256:(c + 1) * 256], preferred_element_type=F32)
            for hh in range(2):
                y = y2[:, hh * LANES:(hh + 1) * LANES]
                out = y * cos + pltpu.roll(y, C_KDIM // 2, 1) * sin
                lo = c * 256 + hh * LANES
                if lo >= C_QK_W:
                    out = out * (C_KDIM ** -0.5)
                o_ref[0, :, lo:lo + LANES] = out.astype(BF16)

    @pl.when(j == 1)
    def _():
        for c in range(width // 256):
            sl = slice(c * 256, (c + 1) * 256)
            o_ref[0, :, sl] = jnp.dot(xn, w_ref[:, sl], preferred_element_type=F32).astype(BF16)

    @pl.when(j == 2)
    def _():
        for c in range(width // 256):
            sl = slice(c * 256, (c + 1) * 256)
            gt = jnp.dot(xn, w_ref[:, sl], preferred_element_type=F32)
            o_ref[0, :, sl] = (gt * jax.nn.sigmoid(gt)).astype(BF16)


def _c_proj(x, gain, w_in, cos, sin):
    t = x.shape[0]
    tm = min(C_TM, t)
    width = 2 * C_QK_W
    tab = pl.BlockSpec((tm, LANES), lambda i, j: (i, 0))
    return pl.pallas_call(
        _c_proj_body,
        grid=(t // tm, 3),
        in_specs=[
            pl.BlockSpec((tm, D_MODEL), lambda i, j: (i, 0)),
            pl.BlockSpec((1, D_MODEL), lambda i, j: (0, 0)),
            pl.BlockSpec((D_MODEL, width), lambda i, j: (0, j)),
            tab, tab,
        ],
        out_specs=pl.BlockSpec((1, tm, width), lambda i, j: (j, i, 0)),
        out_shape=jax.ShapeDtypeStruct((3, t, width), BF16),
        scratch_shapes=[pltpu.VMEM((tm, D_MODEL), BF16)],
        compiler_params=_params("parallel", "arbitrary"),
        name="c_proj",
    )(x, gain[None, :], w_in.astype(BF16), cos, sin)


def _c_ret_body(q_ref, k_ref, v_ref, dec_ref, xi_ref, zeta_ref, cd_ref, y_ref, r_ref):
    n_chunks = q_ref.shape[2] // C_CHUNK
    r_ref[...] = jnp.zeros_like(r_ref)

    def step(n, carry):
        rows = pl.ds(pl.multiple_of(n * C_CHUNK, C_CHUNK), C_CHUNK)
        for hh in range(C_HPS):
            ks = slice(hh * C_KDIM, (hh + 1) * C_KDIM)
            vs = slice(hh * C_VDIM, (hh + 1) * C_VDIM)
            qc = q_ref[0, 0, rows, ks]
            kc = k_ref[0, 0, rows, ks]
            vc = v_ref[0, 0, rows, vs]
            s = lax.dot_general(qc, kc, NT_DIMS, preferred_element_type=F32) * dec_ref[hh]
            inner = jnp.dot(s.astype(BF16), vc, preferred_element_type=F32)
            r_old = r_ref[hh]
            cross = jnp.dot(qc, r_old.astype(BF16), preferred_element_type=F32) * xi_ref[hh]
            kz = (kc.astype(F32) * zeta_ref[hh]).astype(BF16)
            r_ref[hh] = cd_ref[hh] * r_old + lax.dot_general(kz, vc, TN_DIMS, preferred_element_type=F32)
            y = inner + cross
            mu = jnp.mean(y, axis=-1, keepdims=True)
            yc = y - mu
            var = jnp.mean(yc * yc, axis=-1, keepdims=True)
            y_ref[0, rows, vs] = (yc * lax.rsqrt(var + GN_EPS)).astype(BF16)
        return carry

    lax.fori_loop(0, n_chunks, step, 0)


def _c_ret(qkv, batch, seq_len):
    cc = C_CHUNK
    log_g = jnp.log(1.0 - 2.0 ** (-5.0 - jnp.arange(C_HEADS, dtype=F32)))
    idx = jnp.arange(cc, dtype=F32)
    diff = idx[:, None] - idx[None, :]
    causal = diff >= 0
    decay = jnp.where(causal[None], jnp.exp(jnp.where(causal, diff, 0.0)[None] * log_g[:, None, None]), 0.0)
    xi = jnp.exp((idx + 1.0)[None, :] * log_g[:, None])[:, :, None]
    zeta = jnp.exp((cc - 1.0 - idx)[None, :] * log_g[:, None])[:, :, None]
    cdec = jnp.broadcast_to(jnp.exp(cc * log_g)[:, None, None], (C_HEADS, 1, C_VDIM))
    groups = C_HEADS // C_HPS
    per_head = lambda shape: pl.BlockSpec((C_HPS,) + shape, lambda b, h: (h, 0, 0))
    return pl.pallas_call(
        _c_ret_body,
        grid=(batch, groups),
        in_specs=[
            pl.BlockSpec((1, 1, seq_len, C_HPS * C_KDIM), lambda b, h: (0, b, 0, h)),
            pl.BlockSpec((1, 1, seq_len, C_HPS * C_KDIM), lambda b, h: (0, b, 0, groups + h)),
            pl.BlockSpec((1, 1, seq_len, C_HPS * C_VDIM), lambda b, h: (1, b, 0, h)),
            per_head((cc, cc)), per_head((cc, 1)), per_head((cc, 1)), per_head((1, C_VDIM)),
        ],
        out_specs=pl.BlockSpec((1, seq_len, C_HPS * C_VDIM), lambda b, h: (b, 0, h)),
        out_shape=jax.ShapeDtypeStruct((batch, seq_len, C_V_W), BF16),
        scratch_shapes=[pltpu.VMEM((C_HPS, C_KDIM, C_VDIM), F32)],
        compiler_params=_params("parallel", "parallel"),
        name="c_ret",
    )(qkv, qkv, qkv, decay, xi, zeta, cdec)


def _mixer_c(x, gain, batch, seq_len, trig, w_in, w_out):
    t = x.shape[0]
    cos, sin = trig
    tab_c = jnp.concatenate([cos, cos], axis=1)
    tab_s = jnp.concatenate([-sin, sin], axis=1)
    proj = _c_proj(x, gain, w_in, tab_c, tab_s)
    y = _c_ret(proj.reshape(3, batch, seq_len, 2 * C_QK_W), batch, seq_len)
    return y.reshape(t, C_V_W), proj, w_out


D_TS = 256


def _d_body(x_ref, g_ref, win_ref, cw_ref, cb_ref, wrg_ref, brg_ref, wig_ref, big_ref, lru_ref,
            o_ref, ubuf_ref, a_ref, b_ref, hs_ref, h_ref):
    ti = pl.program_id(1)
    ts = x_ref.shape[1]
    pad = SUBLANES

    @pl.when(ti == 0)
    def _():
        ubuf_ref[0:pad, :] = jnp.zeros((pad, D_WIDTH), F32)
        h_ref[...] = jnp.zeros_like(h_ref)

    xn = _rms(x_ref[0], g_ref[...]).astype(BF16)
    gate = jnp.dot(xn, win_ref[:, :D_WIDTH], preferred_element_type=F32)
    u = jnp.dot(xn, win_ref[:, D_WIDTH:], preferred_element_type=F32)
    ubuf_ref[pad:pad + ts, :] = u
    uc = cb_ref[...] + cw_ref[D_CONV - 1:D_CONV, :] * u
    for k in range(D_CONV - 1):
        off = pad - (D_CONV - 1) + k
        uc = uc + cw_ref[k:k + 1, :] * ubuf_ref[off:off + ts, :]
    ubuf_ref[0:pad, :] = ubuf_ref[ts:ts + pad, :]

    ucb = uc.astype(BF16)
    softplus_neg = jax.nn.softplus(-lru_ref[...])
    for n in range(D_BLOCKS):
        sl = slice(n * D_BLOCK, (n + 1) * D_BLOCK)
        r = jax.nn.sigmoid(jnp.dot(ucb[:, sl], wrg_ref[n], preferred_element_type=F32) + brg_ref[:, sl])
        ig = jax.nn.sigmoid(jnp.dot(ucb[:, sl], wig_ref[n], preferred_element_type=F32) + big_ref[:, sl])
        log_a = -LRU_C * r * softplus_neg[:, sl]
        a = jnp.exp(log_a)
        a_ref[:, sl] = a
        b_ref[:, sl] = jnp.sqrt(jnp.maximum(-jnp.tanh(log_a) * (1.0 + a * a), 0.0)) * (ig * uc[:, sl])

    row = lax.broadcasted_iota(jnp.int32, (SUBLANES, D_WIDTH), 0)

    def scan_group(j, h):
        rows = pl.ds(pl.multiple_of(j * SUBLANES, SUBLANES), SUBLANES)
        a = a_ref[rows, :]
        b = b_ref[rows, :]
        for s in (1, 2, 4):
            keep = row >= s
            b = jnp.where(keep, a * pltpu.roll(b, s, 0) + b, b)
            a = jnp.where(keep, a * pltpu.roll(a, s, 0), a)
        hs = a * h + b
        hs_ref[rows, :] = hs
        return jnp.broadcast_to(hs[SUBLANES - 1:SUBLANES, :], (SUBLANES, D_WIDTH))

    h_ref[...] = lax.fori_loop(0, ts // SUBLANES, scan_group, h_ref[...])
    o_ref[0] = (jax.nn.gelu(gate) * hs_ref[...]).astype(BF16)


def _d_main(x3, gain, w_in, conv_w, conv_b, w_rg, b_rg, w_ig, b_ig, lru_param):
    batch, seq_len, _ = x3.shape
    ts = min(D_TS, seq_len)
    vec = lambda a: a[None, :]
    return pl.pallas_call(
        _d_body,
        grid=(batch, seq_len // ts),
        in_specs=[
            pl.BlockSpec((1, ts, D_MODEL), lambda b, i: (b, i, 0)),
            _const_spec((1, D_MODEL)),
            _const_spec((D_MODEL, 2 * D_WIDTH)),
            _const_spec((D_CONV, D_WIDTH)),
            _const_spec((1, D_WIDTH)),
            _const_spec((D_BLOCKS, D_BLOCK, D_BLOCK)),
            _const_spec((1, D_WIDTH)),
            _const_spec((D_BLOCKS, D_BLOCK, D_BLOCK)),
            _const_spec((1, D_WIDTH)),
            _const_spec((1, D_WIDTH)),
        ],
        out_specs=pl.BlockSpec((1, ts, D_WIDTH), lambda b, i: (b, i, 0)),
        out_shape=jax.ShapeDtypeStruct((batch, seq_len, D_WIDTH), BF16),
        scratch_shapes=[
            pltpu.VMEM((ts + SUBLANES, D_WIDTH), F32),
            pltpu.VMEM((ts, D_WIDTH), F32),
            pltpu.VMEM((ts, D_WIDTH), F32),
            pltpu.VMEM((ts, D_WIDTH), F32),
            pltpu.VMEM((SUBLANES, D_WIDTH), F32),
        ],
        compiler_params=_params("parallel", "arbitrary"),
        name="d_main",
    )(x3, vec(gain), w_in.astype(BF16), conv_w, vec(conv_b), w_rg.astype(BF16), vec(b_rg),
      w_ig.astype(BF16), vec(b_ig), vec(lru_param))


def _mixer_d(x, gain, batch, seq_len, w_in, conv_w, conv_b, w_rg, b_rg, w_ig, b_ig, lru_param, w_out):
    t = x.shape[0]
    y = _d_main(x.reshape(batch, seq_len, D_MODEL), gain, w_in, conv_w, conv_b, w_rg, b_rg, w_ig, b_ig, lru_param)
    return y.reshape(t, D_WIDTH), w_out


def kernel(x, positions, norm_gains, ffn_w_in, ffn_w_out, a_w_in, a_q_gain, a_k_gain, a_w_out, b_w_in, b_q_a_gain, b_w_q_up, b_kv_a_gain, b_w_kv_up, b_q_gain, b_k_gain, b_w_out, c_w_in, c_w_out, d_w_in, d_conv_w, d_conv_b, d_w_rg, d_b_rg, d_w_ig, d_b_ig, d_lru_param, d_w_out):
    batch, seq_len, _ = x.shape
    depth = norm_gains.shape[0]
    t = batch * seq_len
    h = x.reshape(t, D_MODEL)
    trig_a = _rope_trig(positions, A_HEAD_DIM // 2)
    trig_b = _rope_trig(positions, B_ROPE // 2)
    trig_c = _rope_trig(positions, C_KDIM // 2)
    for i in range(depth):
        m, j = i % 4, i // 4
        h = _ffn(h, norm_gains[i, 0], ffn_w_in[i, 0], ffn_w_out[i, 0])
        g = norm_gains[i, 1]
        pre = None
        if m == 0:
            h = _mixer_a(h, g, batch, seq_len, trig_a, a_w_in[j], a_q_gain[j], a_k_gain[j], a_w_out[j])
        elif m == 1:
            pre = _mixer_b(h, g, batch, seq_len, trig_b, b_w_in[j], b_q_a_gain[j], b_w_q_up[j], b_kv_a_gain[j],
                           b_w_kv_up[j], b_q_gain[j], b_k_gain[j], b_w_out[j])
        elif m == 2:
            pre = _mixer_c(h, g, batch, seq_len, trig_c, c_w_in[j], c_w_out[j])
        else:
            pre = _mixer_d(h, g, batch, seq_len, d_w_in[j], d_conv_w[j], d_conv_b[j], d_w_rg[j], d_b_rg[j],
                           d_w_ig[j], d_b_ig[j], d_lru_param[j], d_w_out[j])
        h = _ffn(h, norm_gains[i, 2], ffn_w_in[i, 1], ffn_w_out[i, 1], pre=pre)
    return h.reshape(batch, seq_len, D_MODEL)
```

```python
import functools
import math

import jax
import jax.numpy as jnp
from jax import lax
from jax.experimental import pallas as pl
from jax.experimental.pallas import tpu as pltpu

F32 = jnp.float32
BF16 = jnp.bfloat16

D_MODEL = 1024
D_FF = 2816
NORM_EPS = 1e-6
GN_EPS = 1e-5
ROPE_THETA = 10000.0
NEG_INF = -1e30
LOG2E = math.log2(math.e)

A_HEADS = 16
A_HEAD_DIM = 64
A_PATTERNS = ((128, 1), (512, 4), (2048, 16))
A_WINDOW_STEPS = 128
A_GROUP_WIDTH = A_HEADS * A_HEAD_DIM

B_HEADS = 16
B_NOPE = 64
B_ROPE = 32
B_QK = B_NOPE + B_ROPE
B_VDIM = 64
B_Q_RANK = 384
B_KV_RANK = 256

C_HEADS = 8
C_KDIM = 128
C_VDIM = 256
C_CHUNK = 256

D_WIDTH = 1024
D_BLOCKS = 4
D_BLOCK = D_WIDTH // D_BLOCKS
D_CONV = 4
LRU_C = 8.0

LANES = 128
SUBLANES = 8
VMEM_LIMIT_BYTES = 52 * 1024 * 1024

NT_DIMS = (((1,), (1,)), ((), ()))
TN_DIMS = (((0,), (0,)), ((), ()))


def _params(*semantics):
    return pltpu.CompilerParams(dimension_semantics=semantics, vmem_limit_bytes=VMEM_LIMIT_BYTES)


def _rms(x, g):
    ms = jnp.mean(x * x, axis=-1, keepdims=True)
    return x * lax.rsqrt(ms + NORM_EPS) * g


def _const_spec(shape):
    nd = len(shape)
    return pl.BlockSpec(shape, lambda *_: (0,) * nd, pipeline_mode=pl.Buffered(1))


def _trig_body(pos_ref, inv_ref, cos_ref, sin_ref):
    ang = pos_ref[...] * inv_ref[...]
    cos_ref[...] = jnp.cos(ang)
    sin_ref[...] = jnp.sin(ang)


def _rope_trig(positions, half):
    t = positions.size
    per_row = LANES // half
    rows = t // per_row
    pos_rep = jnp.repeat(positions.reshape(rows, per_row).astype(F32), half, axis=1)
    inv = ROPE_THETA ** (-jnp.arange(half, dtype=F32) * 2.0 / (2 * half))
    inv_row = jnp.tile(inv, per_row)[None, :]
    tr = min(rows, 2048)
    cos, sin = pl.pallas_call(
        _trig_body,
        grid=(rows // tr,),
        in_specs=[pl.BlockSpec((tr, LANES), lambda i: (i, 0)), pl.BlockSpec((1, LANES), lambda i: (0, 0))],
        out_specs=[pl.BlockSpec((tr, LANES), lambda i: (i, 0))] * 2,
        out_shape=[jax.ShapeDtypeStruct((rows, LANES), F32)] * 2,
        compiler_params=_params("parallel"),
        name="rope_trig",
    )(pos_rep, inv_row)
    return cos.reshape(t, half), sin.reshape(t, half)


def _rope_gain_rows(gain_row, shift_a, shift_b, scale):
    return jnp.stack([gain_row, jnp.roll(gain_row, shift_a), jnp.roll(gain_row, shift_b)]) * scale


FFN_TM = 512
FFN_TF = 256


def _ffn_body(*refs, pre):
    if pre == "proj":
        a_ref, wp_ref, *refs = refs
        delta = jnp.dot(a_ref[...], wp_ref[...], preferred_element_type=F32)
    elif pre == "gated":
        a_ref, gate_ref, wp_ref, *refs = refs
        a = (a_ref[...].astype(F32) * gate_ref[0].astype(F32)).astype(BF16)
        delta = jnp.dot(a, wp_ref[...], preferred_element_type=F32)
    x_ref, g_ref, wg_ref, wu_ref, wo_ref, o_ref, act_ref = refs
    x = x_ref[...] if pre == "none" else x_ref[...] + delta
    xn = _rms(x, g_ref[...]).astype(BF16)
    for j in range(D_FF // FFN_TF):
        sl = slice(j * FFN_TF, (j + 1) * FFN_TF)
        gate = jnp.dot(xn, wg_ref[:, sl], preferred_element_type=F32)
        up = jnp.dot(xn, wu_ref[:, sl], preferred_element_type=F32)
        act_ref[:, sl] = (gate * jax.nn.sigmoid(gate) * up).astype(BF16)
    y = jnp.dot(act_ref[...], wo_ref[...], preferred_element_type=F32)
    o_ref[...] = x + 0.5 * y


def _ffn(x, gain, w_in, w_out, pre=None):
    t = x.shape[0]
    tm = min(FFN_TM, t)
    wg = w_in[:, :D_FF].astype(BF16)
    wu = w_in[:, D_FF:].astype(BF16)
    wo = w_out.astype(BF16)
    row = lambda i: (i, 0)
    if pre is None:
        kind, pre_args, pre_specs = "none", [], []
    elif len(pre) == 2:
        a, wp = pre
        kind, pre_args = "proj", [a, wp.astype(BF16)]
        pre_specs = [pl.BlockSpec((tm, a.shape[1]), row), _const_spec(wp.shape)]
    else:
        a, gate3, wp = pre
        kind, pre_args = "gated", [a, gate3, wp.astype(BF16)]
        pre_specs = [pl.BlockSpec((tm, a.shape[1]), row),
                     pl.BlockSpec((1, tm, a.shape[1]), lambda i: (2, i, 0)), _const_spec(wp.shape)]
    return pl.pallas_call(
        functools.partial(_ffn_body, pre=kind),
        grid=(t // tm,),
        in_specs=pre_specs + [
            pl.BlockSpec((tm, D_MODEL), row),
            _const_spec((1, D_MODEL)),
            _const_spec((D_MODEL, D_FF)),
            _const_spec((D_MODEL, D_FF)),
            _const_spec((D_FF, D_MODEL)),
        ],
        out_specs=pl.BlockSpec((tm, D_MODEL), row),
        out_shape=jax.ShapeDtypeStruct((t, D_MODEL), F32),
        scratch_shapes=[pltpu.VMEM((tm, D_FF), BF16)],
        compiler_params=_params("parallel"),
        name="ffn",
    )(*pre_args, x, gain[None, :], wg, wu, wo)


A_TM = 512
A_TL = 512
A_OUT_TM = 512
A_QSCALE = A_HEAD_DIM ** -0.5 * LOG2E
A_STAGE_SLOTS = 8


def _a_proj_body(x_ref, g_ref, w_ref, gain_ref, seg_ref, c_ref, sa_ref, sb_ref, o0_ref, o1_ref, o2_ref,
                 xn_ref, stage_ref, ybuf_ref):
    j = pl.program_id(1)
    tm = x_ref.shape[0]
    out_refs = (o0_ref, o1_ref, o2_ref)

    @pl.when(j == 0)
    def _():
        xn_ref[...] = _rms(x_ref[...], g_ref[...]).astype(BF16)

    xn = xn_ref[...]

    def emit(val, col):
        g, within = divmod(col, A_GROUP_WIDTH)
        d = A_PATTERNS[g][1]
        if d == 1:
            out_refs[g][0, 0, :, within:within + LANES] = val.astype(BF16)
            return
        slot = (col // LANES) % A_STAGE_SLOTS
        stage_ref[slot] = val
        for r in range(d):
            rows = stage_ref[slot, pl.ds(r, tm // d, stride=d), :]
            out_refs[g][0, 0, :, r * A_GROUP_WIDTH + within:r * A_GROUP_WIDTH + within + LANES] = rows.astype(BF16)

    @pl.when(j < 2)
    def _():
        tab_c = c_ref[...] * gain_ref[0, 0:1, :]
        tab_a = sa_ref[...] * gain_ref[0, 1:2, :]
        tab_b = sb_ref[...] * gain_ref[0, 2:3, :]
        for c in range(w_ref.shape[1] // 256):
            ybuf_ref[:, c * 256:(c + 1) * 256] = jnp.dot(
                xn, w_ref[:, c * 256:(c + 1) * 256], preferred_element_type=F32)
        for c in range(w_ref.shape[1] // 256):
            y2 = ybuf_ref[:, c * 256:(c + 1) * 256]
            ss = jnp.dot((y2 * y2).astype(BF16), seg_ref[...], preferred_element_type=F32)
            inv = lax.rsqrt(ss + A_HEAD_DIM * NORM_EPS)
            for hh in range(2):
                y = y2[:, hh * LANES:(hh + 1) * LANES]
                rot = y * tab_c + pltpu.roll(y, 96, 1) * tab_a + pltpu.roll(y, 32, 1) * tab_b
                emit(rot * inv[:, hh * LANES:(hh + 1) * LANES], c * 256 + hh * LANES)

    @pl.when(j == 2)
    def _():
        for c in range(w_ref.shape[1] // 256):
            ybuf_ref[:, c * 256:(c + 1) * 256] = jnp.dot(
                xn, w_ref[:, c * 256:(c + 1) * 256], preferred_element_type=F32)
        for c in range(w_ref.shape[1] // LANES):
            emit(ybuf_ref[:, c * LANES:(c + 1) * LANES], c * LANES)


def _a_proj(x, gain, batch, seq_len, w_in, q_gain, k_gain, cos, sin_a, sin_b):
    t = x.shape[0]
    tm = min(A_TM, seq_len)
    tiles = seq_len // tm
    width = 3 * A_GROUP_WIDTH
    root = A_HEAD_DIM ** 0.5
    gains = jnp.stack([_rope_gain_rows(jnp.tile(q_gain, 2), 96, 32, root * A_QSCALE),
                       _rope_gain_rows(jnp.tile(k_gain, 2), 96, 32, root)])
    seg = jnp.kron(jnp.eye(256 // A_HEAD_DIM, dtype=F32), jnp.ones((A_HEAD_DIM, A_HEAD_DIM), F32)).astype(BF16)
    tab = pl.BlockSpec((tm, LANES), lambda i, j: (i, 0))
    out_map = lambda i, j: (j, i // tiles, i % tiles, 0)
    return pl.pallas_call(
        _a_proj_body,
        grid=(t // tm, 3),
        in_specs=[
            pl.BlockSpec((tm, D_MODEL), lambda i, j: (i, 0)),
            pl.BlockSpec((1, D_MODEL), lambda i, j: (0, 0)),
            pl.BlockSpec((D_MODEL, width), lambda i, j: (0, j)),
            pl.BlockSpec((1, 3, LANES), lambda i, j: (jnp.minimum(j, 1), 0, 0)),
            pl.BlockSpec((256, 256), lambda i, j: (0, 0)),
            tab, tab, tab,
        ],
        out_specs=[pl.BlockSpec((1, 1, tm // d, d * A_GROUP_WIDTH), out_map) for _, d in A_PATTERNS],
        out_shape=[jax.ShapeDtypeStruct((3, batch, seq_len // d, d * A_GROUP_WIDTH), BF16) for _, d in A_PATTERNS],
        scratch_shapes=[pltpu.VMEM((tm, D_MODEL), BF16), pltpu.VMEM((A_STAGE_SLOTS, tm, LANES), F32),
                        pltpu.VMEM((tm, width), F32)],
        compiler_params=_params("parallel", "arbitrary"),
        name="a_proj",
    )(x, gain[None, :], w_in.astype(BF16), gains, seg, cos, sin_a, sin_b)


def _a_attn_body(q_ref, kc_ref, kp_ref, vc_ref, vp_ref, o_ref, m_ref, l_ref, *, whole_seq):
    li = pl.program_id(2)
    w = A_WINDOW_STEPS
    nblk = q_ref.shape[2] // w
    even = lax.broadcasted_iota(jnp.int32, (w, LANES), 1) < A_HEAD_DIM
    key = lax.broadcasted_iota(jnp.int32, (2 * w, 2 * w), 0)
    col = lax.broadcasted_iota(jnp.int32, (2 * w, 2 * w), 1)
    qry = jnp.where(col >= w, col - w, col)
    band_bias = jnp.where((key >= qry) & (key <= qry + w), 0.0, NEG_INF)
    if whole_seq:
        key1 = lax.broadcasted_iota(jnp.int32, (w, 2 * w), 0)
        col1 = lax.broadcasted_iota(jnp.int32, (w, 2 * w), 1)
        tri_bias = jnp.where(key1 <= jnp.where(col1 >= w, col1 - w, col1), 0.0, NEG_INF)
    else:
        is_first = jnp.where(li == 0, 1.0, 0.0)
        start_bias = band_bias + jnp.where(key < w, NEG_INF, 0.0) * is_first

    for rr, hp in [(rr, hp) for rr in range(o_ref.shape[2] // A_GROUP_WIDTH) for hp in range(A_HEADS // 2)]:
        ls = slice(rr * A_GROUP_WIDTH + hp * LANES, rr * A_GROUP_WIDTH + (hp + 1) * LANES)
        for j in range(nblk):
            rs = slice(j * w, (j + 1) * w)
            qb = q_ref[0, 0, rs, ls]
            zero = jnp.zeros_like(qb)
            q2 = jnp.concatenate([jnp.where(even, qb, zero), jnp.where(even, zero, qb)], axis=0)
            if j == 0 and whole_seq:
                kk, vv, bias = kc_ref[0, 0, rs, ls], vc_ref[0, 0, rs, ls], tri_bias
            elif j == 0:
                kk = jnp.concatenate([kp_ref[0, 0, :, ls], kc_ref[0, 0, rs, ls]], axis=0)
                vv = jnp.concatenate([vp_ref[0, 0, :, ls], vc_ref[0, 0, rs, ls]], axis=0)
                bias = start_bias
            else:
                kk = kc_ref[0, 0, (j - 1) * w:(j + 1) * w, ls]
                vv = vc_ref[0, 0, (j - 1) * w:(j + 1) * w, ls]
                bias = band_bias
            st = lax.dot_general(kk, q2, NT_DIMS, preferred_element_type=F32) + bias
            probs = []
            for hh in range(2):
                sh = st[:, hh * w:(hh + 1) * w]
                m = jnp.max(sh, axis=0, keepdims=True)
                ph = jnp.exp2(sh - m)
                head = 2 * hp + hh
                m_ref[0, rr, head:head + 1, rs] = m
                l_ref[0, rr, head:head + 1, rs] = jnp.sum(ph, axis=0, keepdims=True)
                probs.append(ph.astype(BF16))
            p = jnp.concatenate(probs, axis=1)
            o2 = lax.dot_general(p, vv, TN_DIMS, preferred_element_type=F32)
            o_ref[0, rs, ls] = jnp.where(even, o2[:w], o2[w:]).astype(BF16)


def _a_attn(qkv, group, batch, seq_len):
    _, d = A_PATTERNS[group]
    sub_len = seq_len // d
    tl = min(A_TL, sub_len)
    w = A_WINDOW_STEPS
    rpb = A_TL // tl
    lanes = rpb * A_GROUP_WIDTH
    cur = lambda which: pl.BlockSpec((1, 1, tl, lanes), lambda b, r, li: (which, b, li, r))
    prev = lambda which: pl.BlockSpec(
        (1, 1, w, lanes), lambda b, r, li: (which, b, jnp.maximum(li * (tl // w) - 1, 0), r))
    stat = pl.BlockSpec((1, rpb, A_HEADS, tl), lambda b, r, li: (b, r, 0, li))
    return pl.pallas_call(
        functools.partial(_a_attn_body, whole_seq=(tl == sub_len)),
        grid=(batch, d // rpb, sub_len // tl),
        in_specs=[cur(0), cur(1), prev(1), cur(2), prev(2)],
        out_specs=[pl.BlockSpec((1, tl, lanes), lambda b, r, li: (b, li, r)), stat, stat],
        out_shape=[
            jax.ShapeDtypeStruct((batch, sub_len, d * A_GROUP_WIDTH), BF16),
            jax.ShapeDtypeStruct((batch, d, A_HEADS, sub_len), F32),
            jax.ShapeDtypeStruct((batch, d, A_HEADS, sub_len), F32),
        ],
        compiler_params=_params("parallel", "parallel", "arbitrary"),
        name="a_attn",
    )(qkv, qkv, qkv, qkv, qkv)


def _a_out_body(x_ref, o0_ref, o1_ref, o2_ref, p1_ref, p2_ref, m_ref, l_ref, e_ref, w_ref, out_ref):
    def token_major(o_ref, perm_ref):
        d = o_ref.shape[2] // A_GROUP_WIDTH
        stacked = jnp.concatenate(
            [o_ref[0, :, r * A_GROUP_WIDTH:(r + 1) * A_GROUP_WIDTH] for r in range(d)], axis=0)
        return jnp.dot(perm_ref[...], stacked, preferred_element_type=F32)

    groups = (o0_ref[0].astype(F32), token_major(o1_ref, p1_ref), token_major(o2_ref, p2_ref))
    m0, m1, m2 = m_ref[0], m_ref[1], m_ref[2]
    top = jnp.maximum(jnp.maximum(m0, m1), m2)
    c0, c1, c2 = jnp.exp2(m0 - top), jnp.exp2(m1 - top), jnp.exp2(m2 - top)
    inv = 1.0 / (c0 * l_ref[0] + c1 * l_ref[1] + c2 * l_ref[2])
    merged = None
    for coef, o in zip((c0, c1, c2), groups):
        wt = coef * inv
        wexp = jnp.dot(wt.astype(BF16), e_ref[...], preferred_element_type=F32)
        merged = wexp * o if merged is None else merged + wexp * o
    out_ref[...] = x_ref[...] + jnp.dot(merged.astype(BF16), w_ref[...], preferred_element_type=F32)


def _a_out(x, outs, m3, l3, batch, seq_len, w_out):
    t = x.shape[0]
    tm = min(A_OUT_TM, seq_len)
    tiles = seq_len // tm
    expand = jnp.repeat(jnp.eye(A_HEADS, dtype=BF16), A_HEAD_DIM, axis=1)
    perms = []
    for _, d in A_PATTERNS[1:]:
        src = jnp.arange(tm)
        dst = (src % (tm // d)) * d + src // (tm // d)
        perms.append(jnp.zeros((tm, tm), BF16).at[dst, src].set(1.0))
    row = lambda i: (i, 0)
    o_specs = [pl.BlockSpec((1, tm // d, d * A_GROUP_WIDTH), lambda i: (i // tiles, i % tiles, 0))
               for _, d in A_PATTERNS]
    s_spec = pl.BlockSpec((3, tm, A_HEADS), lambda i: (0, i, 0))
    return pl.pallas_call(
        _a_out_body,
        grid=(t // tm,),
        in_specs=[pl.BlockSpec((tm, D_MODEL), row), *o_specs, _const_spec((tm, tm)), _const_spec((tm, tm)),
                  s_spec, s_spec, _const_spec((A_HEADS, A_GROUP_WIDTH)), _const_spec((A_GROUP_WIDTH, D_MODEL))],
        out_specs=pl.BlockSpec((tm, D_MODEL), row),
        out_shape=jax.ShapeDtypeStruct((t, D_MODEL), F32),
        compiler_params=_params("parallel"),
        name="a_out",
    )(x, *outs, *perms, m3, l3, expand, w_out.astype(BF16))


def _mixer_a(x, gain, batch, seq_len, trig, w_in, q_gain, k_gain, w_out):
    t = x.shape[0]
    cos, sin = trig
    zero = jnp.zeros_like(sin)
    tab_c = jnp.tile(jnp.concatenate([cos, cos], axis=1), (1, 2))
    tab_sa = jnp.tile(jnp.concatenate([-sin, zero], axis=1), (1, 2))
    tab_sb = jnp.tile(jnp.concatenate([zero, sin], axis=1), (1, 2))
    qkv = _a_proj(x, gain, batch, seq_len, w_in, q_gain, k_gain, tab_c, tab_sa, tab_sb)
    outs, ms, ls = [], [], []
    for gi in range(len(A_PATTERNS)):
        o, m, l = _a_attn(qkv[gi], gi, batch, seq_len)
        outs.append(o)
        ms.append(m.transpose(0, 3, 1, 2).reshape(t, A_HEADS))
        ls.append(l.transpose(0, 3, 1, 2).reshape(t, A_HEADS))
    return _a_out(x, outs, jnp.stack(ms), jnp.stack(ls), batch, seq_len, w_out)


B_TM = 512
B_TQ = 256
B_TK = 256
B_HEAD_PAD = LANES
B_IN_PAD = 768
B_QSCALE = B_QK ** -0.5 * LOG2E


def _b_proj_body(x_ref, g_ref, win_ref, qag_ref, kvag_ref, wq_ref, wkv_ref, qg_ref, kg_ref, seg_ref,
                 c_ref, sa_ref, sb_ref, q_ref, k_ref, vt_ref):
    tm = x_ref.shape[0]
    xn = _rms(x_ref[...], g_ref[...]).astype(BF16)
    h = jnp.dot(xn, win_ref[...], preferred_element_type=F32)
    c_q = h[:, :B_Q_RANK]
    rest = h[:, B_Q_RANK:]
    cq = (_rms(c_q, qag_ref[...])).astype(BF16)
    lane_r = lax.broadcasted_iota(jnp.int32, rest.shape, 1)
    is_kv = lane_r < B_KV_RANK
    ms = jnp.sum(jnp.where(is_kv, rest * rest, 0.0), axis=-1, keepdims=True) * (1.0 / B_KV_RANK)
    ckv = jnp.where(is_kv, rest * lax.rsqrt(ms + NORM_EPS) * kvag_ref[...], rest).astype(BF16)
    cos, sin_a, sin_b = c_ref[...], sa_ref[...], sb_ref[...]
    q_tabs = (cos * qg_ref[0:1, :], sin_a * qg_ref[1:2, :], sin_b * qg_ref[2:3, :])
    k_tabs = (cos * kg_ref[0:1, :], sin_a * kg_ref[1:2, :], sin_b * kg_ref[2:3, :])

    def heads_norm_rope(y2, tabs, out_ref, lo):
        ss = jnp.dot((y2 * y2).astype(BF16), seg_ref[...], preferred_element_type=F32)
        inv = lax.rsqrt(ss + B_QK * NORM_EPS)
        for hh in range(2):
            y = y2[:, hh * LANES:(hh + 1) * LANES]
            rot = (y * tabs[0] + pltpu.roll(y, LANES - B_ROPE // 2, 1) * tabs[1]
                   + pltpu.roll(y, B_ROPE // 2, 1) * tabs[2])
            out_ref[:, lo + hh * LANES:lo + (hh + 1) * LANES] = (
                rot * inv[:, hh * LANES:(hh + 1) * LANES]).astype(BF16)

    for hd in range(B_HEADS // 2):
        sl = slice(hd * 256, (hd + 1) * 256)
        heads_norm_rope(jnp.dot(cq, wq_ref[:, sl], preferred_element_type=F32), q_tabs, q_ref, hd * 256)
        heads_norm_rope(jnp.dot(ckv, wkv_ref[:, sl], preferred_element_type=F32), k_tabs, k_ref, hd * 256)
    k_width = B_HEADS * B_HEAD_PAD
    for c in range(B_HEADS * B_VDIM // 256):
        sl = slice(k_width + c * 256, k_width + (c + 1) * 256)
        vc = jnp.dot(ckv, wkv_ref[:, sl], preferred_element_type=F32).astype(BF16)
        for s in range(tm // B_TK):
            vt_ref[0, s, c * 256:(c + 1) * 256, :] = vc[s * B_TK:(s + 1) * B_TK, :].T


def _b_proj(x, gain, batch, seq_len, w_in, q_a_gain, w_q_up, kv_a_gain, w_kv_up, q_gain, k_gain, cos, sin_a, sin_b):
    t = x.shape[0]
    tm = min(B_TM, seq_len)
    in_w = B_Q_RANK + B_KV_RANK + B_ROPE
    win = jnp.pad(w_in, ((0, 0), (0, B_IN_PAD - in_w))).astype(BF16)
    wq = jnp.pad(w_q_up.reshape(B_Q_RANK, B_HEADS, B_QK), ((0, 0), (0, 0), (0, B_HEAD_PAD - B_QK)))
    wq = wq.reshape(B_Q_RANK, B_HEADS * B_HEAD_PAD).astype(BF16)
    kv_in = B_IN_PAD - B_Q_RANK
    wkv = w_kv_up.reshape(B_KV_RANK, B_HEADS, B_NOPE + B_VDIM)
    wk = jnp.zeros((kv_in, B_HEADS, B_HEAD_PAD), F32)
    wk = wk.at[:B_KV_RANK, :, :B_NOPE].set(wkv[:, :, :B_NOPE])
    wk = wk.at[B_KV_RANK:B_KV_RANK + B_ROPE, :, B_NOPE:B_QK].set(
        jnp.broadcast_to(jnp.eye(B_ROPE, dtype=F32)[:, None, :], (B_ROPE, B_HEADS, B_ROPE)))
    wv = jnp.zeros((kv_in, B_HEADS, B_VDIM), F32).at[:B_KV_RANK].set(wkv[:, :, B_NOPE:])
    wkv_full = jnp.concatenate(
        [wk.reshape(kv_in, B_HEADS * B_HEAD_PAD), wv.reshape(kv_in, B_HEADS * B_VDIM)], axis=1).astype(BF16)
    root = B_QK ** 0.5
    shift_a, shift_b = LANES - B_ROPE // 2, B_ROPE // 2
    pad_gain = lambda gn: jnp.pad(gn, (0, B_HEAD_PAD - B_QK))
    q_rows = _rope_gain_rows(pad_gain(q_gain), shift_a, shift_b, root * B_QSCALE)
    k_rows = _rope_gain_rows(pad_gain(k_gain), shift_a, shift_b, root)
    kvag = jnp.pad(kv_a_gain, (0, kv_in - B_KV_RANK))[None, :]
    seg = jnp.kron(jnp.eye(2, dtype=F32), jnp.ones((B_HEAD_PAD, B_HEAD_PAD), F32)).astype(BF16)
    row = lambda i: (i, 0)
    tab = pl.BlockSpec((tm, LANES), row)
    tiles = seq_len // tm
    return pl.pallas_call(
        _b_proj_body,
        grid=(t // tm,),
        in_specs=[
            pl.BlockSpec((tm, D_MODEL), row),
            _const_spec((1, D_MODEL)),
            _const_spec((D_MODEL, B_IN_PAD)),
            _const_spec((1, B_Q_RANK)),
            _const_spec((1, kv_in)),
            _const_spec((B_Q_RANK, B_HEADS * B_HEAD_PAD)),
            _const_spec((kv_in, B_HEADS * (B_HEAD_PAD + B_VDIM))),
            _const_spec((3, B_HEAD_PAD)),
            _const_spec((3, B_HEAD_PAD)),
            _const_spec((2 * B_HEAD_PAD, 2 * B_HEAD_PAD)),
            tab, tab, tab,
        ],
        out_specs=[
            pl.BlockSpec((tm, B_HEADS * B_HEAD_PAD), row),
            pl.BlockSpec((tm, B_HEADS * B_HEAD_PAD), row),
            pl.BlockSpec((1, tm // B_TK, B_HEADS * B_VDIM, B_TK), lambda i: (i // tiles, i % tiles, 0, 0)),
        ],
        out_shape=[
            jax.ShapeDtypeStruct((t, B_HEADS * B_HEAD_PAD), BF16),
            jax.ShapeDtypeStruct((t, B_HEADS * B_HEAD_PAD), BF16),
            jax.ShapeDtypeStruct((batch, seq_len // B_TK, B_HEADS * B_VDIM, B_TK), BF16),
        ],
        compiler_params=_params("parallel"),
        name="b_proj",
    )(x, gain[None, :], win, q_a_gain[None, :], kvag, wq, wkv_full, q_rows, k_rows, seg, cos, sin_a, sin_b)


def _b_attn_body(q_ref, k_ref, vt_ref, o_ref, m_ref, l_ref, acc_ref, ot_ref):
    seq_len = q_ref.shape[1]
    tq, tk = B_TQ, B_TK
    key = lax.broadcasted_iota(jnp.int32, (tk, tq), 0)
    qry = lax.broadcasted_iota(jnp.int32, (tk, tq), 1)
    diag_bias = jnp.where(key <= qry, 0.0, NEG_INF)
    n = seq_len // tq
    for kj in range(n):
        lo = kj * tq
        for hh in range(2):
            hl = slice(hh * LANES, (hh + 1) * LANES)
            k = k_ref[0, lo:lo + tk, hl]
            vt = vt_ref[0, kj, hh * B_VDIM:(hh + 1) * B_VDIM, :]
            st = lax.dot_general(k, q_ref[0, lo:, hl], NT_DIMS, preferred_element_type=F32)
            diag = st[:, :tq] + diag_bias
            st = diag if kj == n - 1 else jnp.concatenate([diag, st[:, tq:]], axis=1)
            blk_max = jnp.max(st, axis=0, keepdims=True)
            if kj == 0:
                m_new = blk_max
                p = jnp.exp2(st - m_new)
                l = jnp.sum(p, axis=0, keepdims=True)
                acc = jnp.dot(vt, p.astype(BF16), preferred_element_type=F32)
            else:
                m = m_ref[hh, :, lo:]
                m_new = jnp.maximum(m, blk_max)
                alpha = jnp.exp2(m - m_new)
                p = jnp.exp2(st - m_new)
                l = alpha * l_ref[hh, :, lo:] + jnp.sum(p, axis=0, keepdims=True)
                acc = alpha * acc_ref[hh, :, lo:] + jnp.dot(vt, p.astype(BF16), preferred_element_type=F32)
            m_ref[hh, :, lo:] = m_new
            l_ref[hh, :, lo:] = l
            acc_ref[hh, :, lo:] = acc
            ot_ref[hh * B_VDIM:(hh + 1) * B_VDIM, :] = acc_ref[hh, :, lo:lo + tq] * (1.0 / l_ref[hh, :, lo:lo + tq])
        o_ref[0, lo:lo + tq, :] = ot_ref[...].T.astype(BF16)


def _b_attn(q, k, vt, batch, seq_len):
    return pl.pallas_call(
        _b_attn_body,
        grid=(batch, B_HEADS // 2),
        in_specs=[
            pl.BlockSpec((1, seq_len, 2 * B_HEAD_PAD), lambda b, hp: (b, 0, hp)),
            pl.BlockSpec((1, seq_len, 2 * B_HEAD_PAD), lambda b, hp: (b, 0, hp)),
            pl.BlockSpec((1, seq_len // B_TK, 2 * B_VDIM, B_TK), lambda b, hp: (b, 0, hp, 0)),
        ],
        out_specs=pl.BlockSpec((1, seq_len, 2 * B_VDIM), lambda b, hp: (b, 0, hp)),
        out_shape=jax.ShapeDtypeStruct((batch, seq_len, B_HEADS * B_VDIM), BF16),
        scratch_shapes=[
            pltpu.VMEM((2, 1, seq_len), F32),
            pltpu.VMEM((2, 1, seq_len), F32),
            pltpu.VMEM((2, B_VDIM, seq_len), F32),
            pltpu.VMEM((2 * B_VDIM, B_TQ), F32),
        ],
        compiler_params=_params("parallel", "parallel"),
        name="b_attn",
    )(q, k, vt)


def _mixer_b(x, gain, batch, seq_len, trig, w_in, q_a_gain, w_q_up, kv_a_gain, w_kv_up, q_gain, k_gain, w_out):
    t = x.shape[0]
    cos, sin = trig
    one = jnp.ones((t, B_NOPE), F32)
    zero64 = jnp.zeros((t, B_NOPE), F32)
    z16 = jnp.zeros_like(sin)
    tail = jnp.zeros((t, B_HEAD_PAD - B_QK), F32)
    tab_c = jnp.concatenate([one, cos, cos, tail + 1.0], axis=1)
    tab_sa = jnp.concatenate([zero64, -sin, z16, tail], axis=1)
    tab_sb = jnp.concatenate([zero64, z16, sin, tail], axis=1)
    q, k, vt = _b_proj(x, gain, batch, seq_len, w_in, q_a_gain, w_q_up, kv_a_gain, w_kv_up, q_gain, k_gain,
                       tab_c, tab_sa, tab_sb)
    shp = lambda a: a.reshape(batch, seq_len, a.shape[-1])
    o = _b_attn(shp(q), shp(k), vt, batch, seq_len)
    return o.reshape(t, B_HEADS * B_VDIM), w_out


C_TM = 512
C_QK_W = C_HEADS * C_KDIM
C_V_W = C_HEADS * C_VDIM
C_HPS = 4


def _c_proj_body(x_ref, g_ref, w_ref, c_ref, s_ref, o_ref, xn_ref):
    j = pl.program_id(1)
    width = w_ref.shape[1]

    @pl.when(j == 0)
    def _():
        xn_ref[...] = _rms(x_ref[...], g_ref[...]).astype(BF16)

    xn = xn_ref[...]

    @pl.when(j == 0)
    def _():
        cos, sin = c_ref[...], s_ref[...]
        for c in range(width // 256):
            y2 = jnp.dot(xn, w_ref[:, c * 256:(c + 1) * 256], preferred_element_type=F32)
            for hh in range(2):
                y = y2[:, hh * LANES:(hh + 1) * LANES]
                out = y * cos + pltpu.roll(y, C_KDIM // 2, 1) * sin
                lo = c * 256 + hh * LANES
                if lo >= C_QK_W:
                    out = out * (C_KDIM ** -0.5)
                o_ref[0, :, lo:lo + LANES] = out.astype(BF16)

    @pl.when(j == 1)
    def _():
        for c in range(width // 256):
            sl = slice(c * 256, (c + 1) * 256)
            o_ref[0, :, sl] = jnp.dot(xn, w_ref[:, sl], preferred_element_type=F32).astype(BF16)

    @pl.when(j == 2)
    def _():
        for c in range(width // 256):
            sl = slice(c * 256, (c + 1) * 256)
            gt = jnp.dot(xn, w_ref[:, sl], preferred_element_type=F32)
            o_ref[0, :, sl] = (gt * jax.nn.sigmoid(gt)).astype(BF16)


def _c_proj(x, gain, w_in, cos, sin):
    t = x.shape[0]
    tm = min(C_TM, t)
    width = 2 * C_QK_W
    tab = pl.BlockSpec((tm, LANES), lambda i, j: (i, 0))
    return pl.pallas_call(
        _c_proj_body,
        grid=(t // tm, 3),
        in_specs=[
            pl.BlockSpec((tm, D_MODEL), lambda i, j: (i, 0)),
            pl.BlockSpec((1, D_MODEL), lambda i, j: (0, 0)),
            pl.BlockSpec((D_MODEL, width), lambda i, j: (0, j)),
            tab, tab,
        ],
        out_specs=pl.BlockSpec((1, tm, width), lambda i, j: (j, i, 0)),
        out_shape=jax.ShapeDtypeStruct((3, t, width), BF16),
        scratch_shapes=[pltpu.VMEM((tm, D_MODEL), BF16)],
        compiler_params=_params("parallel", "arbitrary"),
        name="c_proj",
    )(x, gain[None, :], w_in.astype(BF16), cos, sin)


def _c_ret_body(q_ref, k_ref, v_ref, dec_ref, xi_ref, zeta_ref, cd_ref, y_ref, r_ref):
    n_chunks = q_ref.shape[2] // C_CHUNK
    r_ref[...] = jnp.zeros_like(r_ref)

    def step(n, carry):
        rows = pl.ds(pl.multiple_of(n * C_CHUNK, C_CHUNK), C_CHUNK)
        for hh in range(C_HPS):
            ks = slice(hh * C_KDIM, (hh + 1) * C_KDIM)
            vs = slice(hh * C_VDIM, (hh + 1) * C_VDIM)
            qc = q_ref[0, 0, rows, ks]
            kc = k_ref[0, 0, rows, ks]
            vc = v_ref[0, 0, rows, vs]
            s = lax.dot_general(qc, kc, NT_DIMS, preferred_element_type=F32) * dec_ref[hh]
            inner = jnp.dot(s.astype(BF16), vc, preferred_element_type=F32)
            r_old = r_ref[hh]
            cross = jnp.dot(qc, r_old.astype(BF16), preferred_element_type=F32) * xi_ref[hh]
            kz = (kc.astype(F32) * zeta_ref[hh]).astype(BF16)
            r_ref[hh] = cd_ref[hh] * r_old + lax.dot_general(kz, vc, TN_DIMS, preferred_element_type=F32)
            y = inner + cross
            mu = jnp.mean(y, axis=-1, keepdims=True)
            yc = y - mu
            var = jnp.mean(yc * yc, axis=-1, keepdims=True)
            y_ref[0, rows, vs] = (yc * lax.rsqrt(var + GN_EPS)).astype(BF16)
        return carry

    lax.fori_loop(0, n_chunks, step, 0)


def _c_ret(qkv, batch, seq_len):
    cc = C_CHUNK
    log_g = jnp.log(1.0 - 2.0 ** (-5.0 - jnp.arange(C_HEADS, dtype=F32)))
    idx = jnp.arange(cc, dtype=F32)
    diff = idx[:, None] - idx[None, :]
    causal = diff >= 0
    decay = jnp.where(causal[None], jnp.exp(jnp.where(causal, diff, 0.0)[None] * log_g[:, None, None]), 0.0)
    xi = jnp.exp((idx + 1.0)[None, :] * log_g[:, None])[:, :, None]
    zeta = jnp.exp((cc - 1.0 - idx)[None, :] * log_g[:, None])[:, :, None]
    cdec = jnp.broadcast_to(jnp.exp(cc * log_g)[:, None, None], (C_HEADS, 1, C_VDIM))
    groups = C_HEADS // C_HPS
    per_head = lambda shape: pl.BlockSpec((C_HPS,) + shape, lambda b, h: (h, 0, 0))
    return pl.pallas_call(
        _c_ret_body,
        grid=(batch, groups),
        in_specs=[
            pl.BlockSpec((1, 1, seq_len, C_HPS * C_KDIM), lambda b, h: (0, b, 0, h)),
            pl.BlockSpec((1, 1, seq_len, C_HPS * C_KDIM), lambda b, h: (0, b, 0, groups + h)),
            pl.BlockSpec((1, 1, seq_len, C_HPS * C_VDIM), lambda b, h: (1, b, 0, h)),
            per_head((cc, cc)), per_head((cc, 1)), per_head((cc, 1)), per_head((1, C_VDIM)),
        ],
        out_specs=pl.BlockSpec((1, seq_len, C_HPS * C_VDIM), lambda b, h: (b, 0, h)),
        out_shape=jax.ShapeDtypeStruct((batch, seq_len, C_V_W), BF16),
        scratch_shapes=[pltpu.VMEM((C_HPS, C_KDIM, C_VDIM), F32)],
        compiler_params=_params("parallel", "parallel"),
        name="c_ret",
    )(qkv, qkv, qkv, decay, xi, zeta, cdec)


def _mixer_c(x, gain, batch, seq_len, trig, w_in, w_out):
    t = x.shape[0]
    cos, sin = trig
    tab_c = jnp.concatenate([cos, cos], axis=1)
    tab_s = jnp.concatenate([-sin, sin], axis=1)
    proj = _c_proj(x, gain, w_in, tab_c, tab_s)
    y = _c_ret(proj.reshape(3, batch, seq_len, 2 * C_QK_W), batch, seq_len)
    return y.reshape(t, C_V_W), proj, w_out


D_TS = 256


def _d_body(x_ref, g_ref, win_ref, cw_ref, cb_ref, wrg_ref, brg_ref, wig_ref, big_ref, lru_ref,
            o_ref, ubuf_ref, a_ref, b_ref, hs_ref, h_ref):
    ti = pl.program_id(1)
    ts = x_ref.shape[1]
    pad = SUBLANES

    @pl.when(ti == 0)
    def _():
        ubuf_ref[0:pad, :] = jnp.zeros((pad, D_WIDTH), F32)
        h_ref[...] = jnp.zeros_like(h_ref)

    xn = _rms(x_ref[0], g_ref[...]).astype(BF16)
    gate = jnp.dot(xn, win_ref[:, :D_WIDTH], preferred_element_type=F32)
    u = jnp.dot(xn, win_ref[:, D_WIDTH:], preferred_element_type=F32)
    ubuf_ref[pad:pad + ts, :] = u
    uc = cb_ref[...] + cw_ref[D_CONV - 1:D_CONV, :] * u
    for k in range(D_CONV - 1):
        off = pad - (D_CONV - 1) + k
        uc = uc + cw_ref[k:k + 1, :] * ubuf_ref[off:off + ts, :]
    ubuf_ref[0:pad, :] = ubuf_ref[ts:ts + pad, :]

    ucb = uc.astype(BF16)
    softplus_neg = jax.nn.softplus(-lru_ref[...])
    for n in range(D_BLOCKS):
        sl = slice(n * D_BLOCK, (n + 1) * D_BLOCK)
        r = jax.nn.sigmoid(jnp.dot(ucb[:, sl], wrg_ref[n], preferred_element_type=F32) + brg_ref[:, sl])
        ig = jax.nn.sigmoid(jnp.dot(ucb[:, sl], wig_ref[n], preferred_element_type=F32) + big_ref[:, sl])
        log_a = -LRU_C * r * softplus_neg[:, sl]
        a = jnp.exp(log_a)
        a_ref[:, sl] = a
        b_ref[:, sl] = jnp.sqrt(jnp.maximum(-jnp.tanh(log_a) * (1.0 + a * a), 0.0)) * (ig * uc[:, sl])

    row = lax.broadcasted_iota(jnp.int32, (SUBLANES, D_WIDTH), 0)

    def scan_group(j, h):
        rows = pl.ds(pl.multiple_of(j * SUBLANES, SUBLANES), SUBLANES)
        a = a_ref[rows, :]
        b = b_ref[rows, :]
        for s in (1, 2, 4):
            keep = row >= s
            b = jnp.where(keep, a * pltpu.roll(b, s, 0) + b, b)
            a = jnp.where(keep, a * pltpu.roll(a, s, 0), a)
        hs = a * h + b
        hs_ref[rows, :] = hs
        return jnp.broadcast_to(hs[SUBLANES - 1:SUBLANES, :], (SUBLANES, D_WIDTH))

    h_ref[...] = lax.fori_loop(0, ts // SUBLANES, scan_group, h_ref[...])
    o_ref[0] = (jax.nn.gelu(gate) * hs_ref[...]).astype(BF16)


def _d_main(x3, gain, w_in, conv_w, conv_b, w_rg, b_rg, w_ig, b_ig, lru_param):
    batch, seq_len, _ = x3.shape
    ts = min(D_TS, seq_len)
    vec = lambda a: a[None, :]
    return pl.pallas_call(
        _d_body,
        grid=(batch, seq_len // ts),
        in_specs=[
            pl.BlockSpec((1, ts, D_MODEL), lambda b, i: (b, i, 0)),
            _const_spec((1, D_MODEL)),
            _const_spec((D_MODEL, 2 * D_WIDTH)),
            _const_spec((D_CONV, D_WIDTH)),
            _const_spec((1, D_WIDTH)),
            _const_spec((D_BLOCKS, D_BLOCK, D_BLOCK)),
            _const_spec((1, D_WIDTH)),
            _const_spec((D_BLOCKS, D_BLOCK, D_BLOCK)),
            _const_spec((1, D_WIDTH)),
            _const_spec((1, D_WIDTH)),
        ],
        out_specs=pl.BlockSpec((1, ts, D_WIDTH), lambda b, i: (b, i, 0)),
        out_shape=jax.ShapeDtypeStruct((batch, seq_len, D_WIDTH), BF16),
        scratch_shapes=[
            pltpu.VMEM((ts + SUBLANES, D_WIDTH), F32),
            pltpu.VMEM((ts, D_WIDTH), F32),
            pltpu.VMEM((ts, D_WIDTH), F32),
            pltpu.VMEM((ts, D_WIDTH), F32),
            pltpu.VMEM((SUBLANES, D_WIDTH), F32),
        ],
        compiler_params=_params("parallel", "arbitrary"),
        name="d_main",
    )(x3, vec(gain), w_in.astype(BF16), conv_w, vec(conv_b), w_rg.astype(BF16), vec(b_rg),
      w_ig.astype(BF16), vec(b_ig), vec(lru_param))


def _mixer_d(x, gain, batch, seq_len, w_in, conv_w, conv_b, w_rg, b_rg, w_ig, b_ig, lru_param, w_out):
    t = x.shape[0]
    y = _d_main(x.reshape(batch, seq_len, D_MODEL), gain, w_in, conv_w, conv_b, w_rg, b_rg, w_ig, b_ig, lru_param)
    return y.reshape(t, D_WIDTH), w_out


def kernel(x, positions, norm_gains, ffn_w_in, ffn_w_out, a_w_in, a_q_gain, a_k_gain, a_w_out, b_w_in, b_q_a_gain, b_w_q_up, b_kv_a_gain, b_w_kv_up, b_q_gain, b_k_gain, b_w_out, c_w_in, c_w_out, d_w_in, d_conv_w, d_conv_b, d_w_rg, d_b_rg, d_w_ig, d_b_ig, d_lru_param, d_w_out):
    batch, seq_len, _ = x.shape
    depth = norm_gains.shape[0]
    t = batch * seq_len
    h = x.reshape(t, D_MODEL)
    trig_a = _rope_trig(positions, A_HEAD_DIM // 2)
    trig_b = _rope_trig(positions, B_ROPE // 2)
    trig_c = _rope_trig(positions, C_KDIM // 2)
    for i in range(depth):
        m, j = i % 4, i // 4
        h = _ffn(h, norm_gains[i, 0], ffn_w_in[i, 0], ffn_w_out[i, 0])
        g = norm_gains[i, 1]
        pre = None
        if m == 0:
            h = _mixer_a(h, g, batch, seq_len, trig_a, a_w_in[j], a_q_gain[j], a_k_gain[j], a_w_out[j])
        elif m == 1:
            pre = _mixer_b(h, g, batch, seq_len, trig_b, b_w_in[j], b_q_a_gain[j], b_w_q_up[j], b_kv_a_gain[j],
                           b_w_kv_up[j], b_q_gain[j], b_k_gain[j], b_w_out[j])
        elif m == 2:
            pre = _mixer_c(h, g, batch, seq_len, trig_c, c_w_in[j], c_w_out[j])
        else:
            pre = _mixer_d(h, g, batch, seq_len, d_w_in[j], d_conv_w[j], d_conv_b[j], d_w_rg[j], d_b_rg[j],
                           d_w_ig[j], d_b_ig[j], d_lru_param[j], d_w_out[j])
        h = _ffn(h, norm_gains[i, 2], ffn_w_in[i, 1], ffn_w_out[i, 1], pre=pre)
    return h.reshape(batch, seq_len, D_MODEL)
```

```python
import functools
import math

import jax
import jax.numpy as jnp
from jax import lax
from jax.experimental import pallas as pl
from jax.experimental.pallas import tpu as pltpu

F32 = jnp.float32
BF16 = jnp.bfloat16

D_MODEL = 1024
D_FF = 2816
NORM_EPS = 1e-6
GN_EPS = 1e-5
ROPE_THETA = 10000.0
NEG_INF = -1e30
LOG2E = math.log2(math.e)

A_HEADS = 16
A_HEAD_DIM = 64
A_PATTERNS = ((128, 1), (512, 4), (2048, 16))
A_WINDOW_STEPS = 128
A_GROUP_WIDTH = A_HEADS * A_HEAD_DIM

B_HEADS = 16
B_NOPE = 64
B_ROPE = 32
B_QK = B_NOPE + B_ROPE
B_VDIM = 64
B_Q_RANK = 384
B_KV_RANK = 256

C_HEADS = 8
C_KDIM = 128
C_VDIM = 256
C_CHUNK = 128

D_WIDTH = 1024
D_BLOCKS = 4
D_BLOCK = D_WIDTH // D_BLOCKS
D_CONV = 4
LRU_C = 8.0

LANES = 128
SUBLANES = 8
VMEM_LIMIT_BYTES = 52 * 1024 * 1024

NT_DIMS = (((1,), (1,)), ((), ()))
TN_DIMS = (((0,), (0,)), ((), ()))


def _params(*semantics):
    return pltpu.CompilerParams(dimension_semantics=semantics, vmem_limit_bytes=VMEM_LIMIT_BYTES)


def _rms(x, g):
    ms = jnp.mean(x * x, axis=-1, keepdims=True)
    return x * lax.rsqrt(ms + NORM_EPS) * g


def _const_spec(shape):
    nd = len(shape)
    return pl.BlockSpec(shape, lambda *_: (0,) * nd, pipeline_mode=pl.Buffered(1))


def _trig_body(pos_ref, inv_ref, cos_ref, sin_ref):
    ang = pos_ref[...] * inv_ref[...]
    cos_ref[...] = jnp.cos(ang)
    sin_ref[...] = jnp.sin(ang)


def _rope_trig(positions, half):
    t = positions.size
    per_row = LANES // half
    rows = t // per_row
    pos_rep = jnp.repeat(positions.reshape(rows, per_row).astype(F32), half, axis=1)
    inv = ROPE_THETA ** (-jnp.arange(half, dtype=F32) * 2.0 / (2 * half))
    inv_row = jnp.tile(inv, per_row)[None, :]
    tr = min(rows, 2048)
    cos, sin = pl.pallas_call(
        _trig_body,
        grid=(rows // tr,),
        in_specs=[pl.BlockSpec((tr, LANES), lambda i: (i, 0)), pl.BlockSpec((1, LANES), lambda i: (0, 0))],
        out_specs=[pl.BlockSpec((tr, LANES), lambda i: (i, 0))] * 2,
        out_shape=[jax.ShapeDtypeStruct((rows, LANES), F32)] * 2,
        compiler_params=_params("parallel"),
        name="rope_trig",
    )(pos_rep, inv_row)
    return cos.reshape(t, half), sin.reshape(t, half)


def _rope_gain_rows(gain_row, shift_a, shift_b, scale):
    return jnp.stack([gain_row, jnp.roll(gain_row, shift_a), jnp.roll(gain_row, shift_b)]) * scale


FFN_TM = 512
FFN_TF = 256


def _ffn_body(*refs, pre):
    if pre == "proj":
        a_ref, wp_ref, *refs = refs
        delta = jnp.dot(a_ref[...], wp_ref[...], preferred_element_type=F32)
    elif pre == "gated":
        a_ref, gate_ref, wp_ref, *refs = refs
        a = (a_ref[...].astype(F32) * gate_ref[0].astype(F32)).astype(BF16)
        delta = jnp.dot(a, wp_ref[...], preferred_element_type=F32)
    x_ref, g_ref, wg_ref, wu_ref, wo_ref, o_ref, act_ref = refs
    x = x_ref[...] if pre == "none" else x_ref[...] + delta
    xn = _rms(x, g_ref[...]).astype(BF16)
    for j in range(D_FF // FFN_TF):
        sl = slice(j * FFN_TF, (j + 1) * FFN_TF)
        gate = jnp.dot(xn, wg_ref[:, sl], preferred_element_type=F32)
        up = jnp.dot(xn, wu_ref[:, sl], preferred_element_type=F32)
        act_ref[:, sl] = (gate * jax.nn.sigmoid(gate) * up).astype(BF16)
    y = jnp.dot(act_ref[...], wo_ref[...], preferred_element_type=F32)
    o_ref[...] = x + 0.5 * y


def _ffn(x, gain, w_in, w_out, pre=None):
    t = x.shape[0]
    tm = min(FFN_TM, t)
    wg = w_in[:, :D_FF].astype(BF16)
    wu = w_in[:, D_FF:].astype(BF16)
    wo = w_out.astype(BF16)
    row = lambda i: (i, 0)
    if pre is None:
        kind, pre_args, pre_specs = "none", [], []
    elif len(pre) == 2:
        a, wp = pre
        kind, pre_args = "proj", [a, wp.astype(BF16)]
        pre_specs = [pl.BlockSpec((tm, a.shape[1]), row), _const_spec(wp.shape)]
    else:
        a, gate3, wp = pre
        kind, pre_args = "gated", [a, gate3, wp.astype(BF16)]
        pre_specs = [pl.BlockSpec((tm, a.shape[1]), row),
                     pl.BlockSpec((1, tm, a.shape[1]), lambda i: (2, i, 0)), _const_spec(wp.shape)]
    return pl.pallas_call(
        functools.partial(_ffn_body, pre=kind),
        grid=(t // tm,),
        in_specs=pre_specs + [
            pl.BlockSpec((tm, D_MODEL), row),
            _const_spec((1, D_MODEL)),
            _const_spec((D_MODEL, D_FF)),
            _const_spec((D_MODEL, D_FF)),
            _const_spec((D_FF, D_MODEL)),
        ],
        out_specs=pl.BlockSpec((tm, D_MODEL), row),
        out_shape=jax.ShapeDtypeStruct((t, D_MODEL), F32),
        scratch_shapes=[pltpu.VMEM((tm, D_FF), BF16)],
        compiler_params=_params("parallel"),
        name="ffn",
    )(*pre_args, x, gain[None, :], wg, wu, wo)


A_TM = 512
A_TL = 512
A_OUT_TM = 512
A_QSCALE = A_HEAD_DIM ** -0.5 * LOG2E
A_STAGE_SLOTS = 8


def _a_proj_body(x_ref, g_ref, w_ref, gain_ref, seg_ref, c_ref, sa_ref, sb_ref, o0_ref, o1_ref, o2_ref,
                 xn_ref, stage_ref, ybuf_ref):
    j = pl.program_id(1)
    tm = x_ref.shape[0]
    out_refs = (o0_ref, o1_ref, o2_ref)

    @pl.when(j == 0)
    def _():
        xn_ref[...] = _rms(x_ref[...], g_ref[...]).astype(BF16)

    xn = xn_ref[...]

    def emit(val, col):
        g, within = divmod(col, A_GROUP_WIDTH)
        d = A_PATTERNS[g][1]
        if d == 1:
            out_refs[g][0, 0, :, within:within + LANES] = val.astype(BF16)
            return
        slot = (col // LANES) % A_STAGE_SLOTS
        stage_ref[slot] = val
        for r in range(d):
            rows = stage_ref[slot, pl.ds(r, tm // d, stride=d), :]
            out_refs[g][0, 0, :, r * A_GROUP_WIDTH + within:r * A_GROUP_WIDTH + within + LANES] = rows.astype(BF16)

    @pl.when(j < 2)
    def _():
        tab_c = c_ref[...] * gain_ref[0, 0:1, :]
        tab_a = sa_ref[...] * gain_ref[0, 1:2, :]
        tab_b = sb_ref[...] * gain_ref[0, 2:3, :]
        for c in range(w_ref.shape[1] // 256):
            ybuf_ref[:, c * 256:(c + 1) * 256] = jnp.dot(
                xn, w_ref[:, c * 256:(c + 1) * 256], preferred_element_type=F32)
        for c in range(w_ref.shape[1] // 256):
            y2 = ybuf_ref[:, c * 256:(c + 1) * 256]
            ss = jnp.dot((y2 * y2).astype(BF16), seg_ref[...], preferred_element_type=F32)
            inv = lax.rsqrt(ss + A_HEAD_DIM * NORM_EPS)
            for hh in range(2):
                y = y2[:, hh * LANES:(hh + 1) * LANES]
                rot = y * tab_c + pltpu.roll(y, 96, 1) * tab_a + pltpu.roll(y, 32, 1) * tab_b
                emit(rot * inv[:, hh * LANES:(hh + 1) * LANES], c * 256 + hh * LANES)

    @pl.when(j == 2)
    def _():
        for c in range(w_ref.shape[1] // 256):
            ybuf_ref[:, c * 256:(c + 1) * 256] = jnp.dot(
                xn, w_ref[:, c * 256:(c + 1) * 256], preferred_element_type=F32)
        for c in range(w_ref.shape[1] // LANES):
            emit(ybuf_ref[:, c * LANES:(c + 1) * LANES], c * LANES)


def _a_proj(x, gain, batch, seq_len, w_in, q_gain, k_gain, cos, sin_a, sin_b):
    t = x.shape[0]
    tm = min(A_TM, seq_len)
    tiles = seq_len // tm
    width = 3 * A_GROUP_WIDTH
    root = A_HEAD_DIM ** 0.5
    gains = jnp.stack([_rope_gain_rows(jnp.tile(q_gain, 2), 96, 32, root * A_QSCALE),
                       _rope_gain_rows(jnp.tile(k_gain, 2), 96, 32, root)])
    seg = jnp.kron(jnp.eye(256 // A_HEAD_DIM, dtype=F32), jnp.ones((A_HEAD_DIM, A_HEAD_DIM), F32)).astype(BF16)
    tab = pl.BlockSpec((tm, LANES), lambda i, j: (i, 0))
    out_map = lambda i, j: (j, i // tiles, i % tiles, 0)
    return pl.pallas_call(
        _a_proj_body,
        grid=(t // tm, 3),
        in_specs=[
            pl.BlockSpec((tm, D_MODEL), lambda i, j: (i, 0)),
            pl.BlockSpec((1, D_MODEL), lambda i, j: (0, 0)),
            pl.BlockSpec((D_MODEL, width), lambda i, j: (0, j)),
            pl.BlockSpec((1, 3, LANES), lambda i, j: (jnp.minimum(j, 1), 0, 0)),
            pl.BlockSpec((256, 256), lambda i, j: (0, 0)),
            tab, tab, tab,
        ],
        out_specs=[pl.BlockSpec((1, 1, tm // d, d * A_GROUP_WIDTH), out_map) for _, d in A_PATTERNS],
        out_shape=[jax.ShapeDtypeStruct((3, batch, seq_len // d, d * A_GROUP_WIDTH), BF16) for _, d in A_PATTERNS],
        scratch_shapes=[pltpu.VMEM((tm, D_MODEL), BF16), pltpu.VMEM((A_STAGE_SLOTS, tm, LANES), F32),
                        pltpu.VMEM((tm, width), F32)],
        compiler_params=_params("parallel", "arbitrary"),
        name="a_proj",
    )(x, gain[None, :], w_in.astype(BF16), gains, seg, cos, sin_a, sin_b)


def _a_attn_body(q_ref, kc_ref, kp_ref, vc_ref, vp_ref, o_ref, m_ref, l_ref, *, whole_seq):
    li = pl.program_id(2)
    w = A_WINDOW_STEPS
    nblk = q_ref.shape[2] // w
    even = lax.broadcasted_iota(jnp.int32, (w, LANES), 1) < A_HEAD_DIM
    key = lax.broadcasted_iota(jnp.int32, (2 * w, 2 * w), 0)
    col = lax.broadcasted_iota(jnp.int32, (2 * w, 2 * w), 1)
    qry = jnp.where(col >= w, col - w, col)
    band_bias = jnp.where((key >= qry) & (key <= qry + w), 0.0, NEG_INF)
    if whole_seq:
        key1 = lax.broadcasted_iota(jnp.int32, (w, 2 * w), 0)
        col1 = lax.broadcasted_iota(jnp.int32, (w, 2 * w), 1)
        tri_bias = jnp.where(key1 <= jnp.where(col1 >= w, col1 - w, col1), 0.0, NEG_INF)
    else:
        is_first = jnp.where(li == 0, 1.0, 0.0)
        start_bias = band_bias + jnp.where(key < w, NEG_INF, 0.0) * is_first

    for rr, hp in [(rr, hp) for rr in range(o_ref.shape[2] // A_GROUP_WIDTH) for hp in range(A_HEADS // 2)]:
        ls = slice(rr * A_GROUP_WIDTH + hp * LANES, rr * A_GROUP_WIDTH + (hp + 1) * LANES)
        for j in range(nblk):
            rs = slice(j * w, (j + 1) * w)
            qb = q_ref[0, 0, rs, ls]
            zero = jnp.zeros_like(qb)
            q2 = jnp.concatenate([jnp.where(even, qb, zero), jnp.where(even, zero, qb)], axis=0)
            if j == 0 and whole_seq:
                kk, vv, bias = kc_ref[0, 0, rs, ls], vc_ref[0, 0, rs, ls], tri_bias
            elif j == 0:
                kk = jnp.concatenate([kp_ref[0, 0, :, ls], kc_ref[0, 0, rs, ls]], axis=0)
                vv = jnp.concatenate([vp_ref[0, 0, :, ls], vc_ref[0, 0, rs, ls]], axis=0)
                bias = start_bias
            else:
                kk = kc_ref[0, 0, (j - 1) * w:(j + 1) * w, ls]
                vv = vc_ref[0, 0, (j - 1) * w:(j + 1) * w, ls]
                bias = band_bias
            st = lax.dot_general(kk, q2, NT_DIMS, preferred_element_type=F32) + bias
            probs = []
            for hh in range(2):
                sh = st[:, hh * w:(hh + 1) * w]
                m = jnp.max(sh, axis=0, keepdims=True)
                ph = jnp.exp2(sh - m)
                head = 2 * hp + hh
                m_ref[0, rr, head:head + 1, rs] = m
                l_ref[0, rr, head:head + 1, rs] = jnp.sum(ph, axis=0, keepdims=True)
                probs.append(ph.astype(BF16))
            p = jnp.concatenate(probs, axis=1)
            o2 = lax.dot_general(p, vv, TN_DIMS, preferred_element_type=F32)
            o_ref[0, rs, ls] = jnp.where(even, o2[:w], o2[w:]).astype(BF16)


def _a_attn(qkv, group, batch, seq_len):
    _, d = A_PATTERNS[group]
    sub_len = seq_len // d
    tl = min(A_TL, sub_len)
    w = A_WINDOW_STEPS
    rpb = A_TL // tl
    lanes = rpb * A_GROUP_WIDTH
    cur = lambda which: pl.BlockSpec((1, 1, tl, lanes), lambda b, r, li: (which, b, li, r))
    prev = lambda which: pl.BlockSpec(
        (1, 1, w, lanes), lambda b, r, li: (which, b, jnp.maximum(li * (tl // w) - 1, 0), r))
    stat = pl.BlockSpec((1, rpb, A_HEADS, tl), lambda b, r, li: (b, r, 0, li))
    return pl.pallas_call(
        functools.partial(_a_attn_body, whole_seq=(tl == sub_len)),
        grid=(batch, d // rpb, sub_len // tl),
        in_specs=[cur(0), cur(1), prev(1), cur(2), prev(2)],
        out_specs=[pl.BlockSpec((1, tl, lanes), lambda b, r, li: (b, li, r)), stat, stat],
        out_shape=[
            jax.ShapeDtypeStruct((batch, sub_len, d * A_GROUP_WIDTH), BF16),
            jax.ShapeDtypeStruct((batch, d, A_HEADS, sub_len), F32),
            jax.ShapeDtypeStruct((batch, d, A_HEADS, sub_len), F32),
        ],
        compiler_params=_params("parallel", "parallel", "arbitrary"),
        name="a_attn",
    )(qkv, qkv, qkv, qkv, qkv)


def _a_out_body(x_ref, o0_ref, o1_ref, o2_ref, p1_ref, p2_ref, m_ref, l_ref, e_ref, w_ref, out_ref):
    def token_major(o_ref, perm_ref):
        d = o_ref.shape[2] // A_GROUP_WIDTH
        stacked = jnp.concatenate(
            [o_ref[0, :, r * A_GROUP_WIDTH:(r + 1) * A_GROUP_WIDTH] for r in range(d)], axis=0)
        return jnp.dot(perm_ref[...], stacked, preferred_element_type=F32)

    groups = (o0_ref[0].astype(F32), token_major(o1_ref, p1_ref), token_major(o2_ref, p2_ref))
    m0, m1, m2 = m_ref[0], m_ref[1], m_ref[2]
    top = jnp.maximum(jnp.maximum(m0, m1), m2)
    c0, c1, c2 = jnp.exp2(m0 - top), jnp.exp2(m1 - top), jnp.exp2(m2 - top)
    inv = 1.0 / (c0 * l_ref[0] + c1 * l_ref[1] + c2 * l_ref[2])
    merged = None
    for coef, o in zip((c0, c1, c2), groups):
        wt = coef * inv
        wexp = jnp.dot(wt.astype(BF16), e_ref[...], preferred_element_type=F32)
        merged = wexp * o if merged is None else merged + wexp * o
    out_ref[...] = x_ref[...] + jnp.dot(merged.astype(BF16), w_ref[...], preferred_element_type=F32)


def _a_out(x, outs, m3, l3, batch, seq_len, w_out):
    t = x.shape[0]
    tm = min(A_OUT_TM, seq_len)
    tiles = seq_len // tm
    expand = jnp.repeat(jnp.eye(A_HEADS, dtype=BF16), A_HEAD_DIM, axis=1)
    perms = []
    for _, d in A_PATTERNS[1:]:
        src = jnp.arange(tm)
        dst = (src % (tm // d)) * d + src // (tm // d)
        perms.append(jnp.zeros((tm, tm), BF16).at[dst, src].set(1.0))
    row = lambda i: (i, 0)
    o_specs = [pl.BlockSpec((1, tm // d, d * A_GROUP_WIDTH), lambda i: (i // tiles, i % tiles, 0))
               for _, d in A_PATTERNS]
    s_spec = pl.BlockSpec((3, tm, A_HEADS), lambda i: (0, i, 0))
    return pl.pallas_call(
        _a_out_body,
        grid=(t // tm,),
        in_specs=[pl.BlockSpec((tm, D_MODEL), row), *o_specs, _const_spec((tm, tm)), _const_spec((tm, tm)),
                  s_spec, s_spec, _const_spec((A_HEADS, A_GROUP_WIDTH)), _const_spec((A_GROUP_WIDTH, D_MODEL))],
        out_specs=pl.BlockSpec((tm, D_MODEL), row),
        out_shape=jax.ShapeDtypeStruct((t, D_MODEL), F32),
        compiler_params=_params("parallel"),
        name="a_out",
    )(x, *outs, *perms, m3, l3, expand, w_out.astype(BF16))


def _mixer_a(x, gain, batch, seq_len, trig, w_in, q_gain, k_gain, w_out):
    t = x.shape[0]
    cos, sin = trig
    zero = jnp.zeros_like(sin)
    tab_c = jnp.tile(jnp.concatenate([cos, cos], axis=1), (1, 2))
    tab_sa = jnp.tile(jnp.concatenate([-sin, zero], axis=1), (1, 2))
    tab_sb = jnp.tile(jnp.concatenate([zero, sin], axis=1), (1, 2))
    qkv = _a_proj(x, gain, batch, seq_len, w_in, q_gain, k_gain, tab_c, tab_sa, tab_sb)
    outs, ms, ls = [], [], []
    for gi in range(len(A_PATTERNS)):
        o, m, l = _a_attn(qkv[gi], gi, batch, seq_len)
        outs.append(o)
        ms.append(m.transpose(0, 3, 1, 2).reshape(t, A_HEADS))
        ls.append(l.transpose(0, 3, 1, 2).reshape(t, A_HEADS))
    return _a_out(x, outs, jnp.stack(ms), jnp.stack(ls), batch, seq_len, w_out)


B_TM = 512
B_TQ = 512
B_TK = 512
B_HEAD_PAD = LANES
B_IN_PAD = 768
B_QSCALE = B_QK ** -0.5 * LOG2E


def _b_proj_body(x_ref, g_ref, win_ref, qag_ref, kvag_ref, wq_ref, wkv_ref, qg_ref, kg_ref, seg_ref,
                 c_ref, sa_ref, sb_ref, q_ref, k_ref, vt_ref):
    tm = x_ref.shape[0]
    xn = _rms(x_ref[...], g_ref[...]).astype(BF16)
    h = jnp.dot(xn, win_ref[...], preferred_element_type=F32)
    c_q = h[:, :B_Q_RANK]
    rest = h[:, B_Q_RANK:]
    cq = (_rms(c_q, qag_ref[...])).astype(BF16)
    lane_r = lax.broadcasted_iota(jnp.int32, rest.shape, 1)
    is_kv = lane_r < B_KV_RANK
    ms = jnp.sum(jnp.where(is_kv, rest * rest, 0.0), axis=-1, keepdims=True) * (1.0 / B_KV_RANK)
    ckv = jnp.where(is_kv, rest * lax.rsqrt(ms + NORM_EPS) * kvag_ref[...], rest).astype(BF16)
    cos, sin_a, sin_b = c_ref[...], sa_ref[...], sb_ref[...]
    q_tabs = (cos * qg_ref[0:1, :], sin_a * qg_ref[1:2, :], sin_b * qg_ref[2:3, :])
    k_tabs = (cos * kg_ref[0:1, :], sin_a * kg_ref[1:2, :], sin_b * kg_ref[2:3, :])

    def heads_norm_rope(y2, tabs, out_ref, lo):
        ss = jnp.dot((y2 * y2).astype(BF16), seg_ref[...], preferred_element_type=F32)
        inv = lax.rsqrt(ss + B_QK * NORM_EPS)
        for hh in range(2):
            y = y2[:, hh * LANES:(hh + 1) * LANES]
            rot = (y * tabs[0] + pltpu.roll(y, LANES - B_ROPE // 2, 1) * tabs[1]
                   + pltpu.roll(y, B_ROPE // 2, 1) * tabs[2])
            out_ref[:, lo + hh * LANES:lo + (hh + 1) * LANES] = (
                rot * inv[:, hh * LANES:(hh + 1) * LANES]).astype(BF16)

    for hd in range(B_HEADS // 2):
        sl = slice(hd * 256, (hd + 1) * 256)
        heads_norm_rope(jnp.dot(cq, wq_ref[:, sl], preferred_element_type=F32), q_tabs, q_ref, hd * 256)
        heads_norm_rope(jnp.dot(ckv, wkv_ref[:, sl], preferred_element_type=F32), k_tabs, k_ref, hd * 256)
    k_width = B_HEADS * B_HEAD_PAD
    for c in range(B_HEADS * B_VDIM // 256):
        sl = slice(k_width + c * 256, k_width + (c + 1) * 256)
        vc = jnp.dot(ckv, wkv_ref[:, sl], preferred_element_type=F32).astype(BF16)
        for s in range(tm // B_TK):
            vt_ref[0, s, c * 256:(c + 1) * 256, :] = vc[s * B_TK:(s + 1) * B_TK, :].T


def _b_proj(x, gain, batch, seq_len, w_in, q_a_gain, w_q_up, kv_a_gain, w_kv_up, q_gain, k_gain, cos, sin_a, sin_b):
    t = x.shape[0]
    tm = min(B_TM, seq_len)
    in_w = B_Q_RANK + B_KV_RANK + B_ROPE
    win = jnp.pad(w_in, ((0, 0), (0, B_IN_PAD - in_w))).astype(BF16)
    wq = jnp.pad(w_q_up.reshape(B_Q_RANK, B_HEADS, B_QK), ((0, 0), (0, 0), (0, B_HEAD_PAD - B_QK)))
    wq = wq.reshape(B_Q_RANK, B_HEADS * B_HEAD_PAD).astype(BF16)
    kv_in = B_IN_PAD - B_Q_RANK
    wkv = w_kv_up.reshape(B_KV_RANK, B_HEADS, B_NOPE + B_VDIM)
    wk = jnp.zeros((kv_in, B_HEADS, B_HEAD_PAD), F32)
    wk = wk.at[:B_KV_RANK, :, :B_NOPE].set(wkv[:, :, :B_NOPE])
    wk = wk.at[B_KV_RANK:B_KV_RANK + B_ROPE, :, B_NOPE:B_QK].set(
        jnp.broadcast_to(jnp.eye(B_ROPE, dtype=F32)[:, None, :], (B_ROPE, B_HEADS, B_ROPE)))
    wv = jnp.zeros((kv_in, B_HEADS, B_VDIM), F32).at[:B_KV_RANK].set(wkv[:, :, B_NOPE:])
    wkv_full = jnp.concatenate(
        [wk.reshape(kv_in, B_HEADS * B_HEAD_PAD), wv.reshape(kv_in, B_HEADS * B_VDIM)], axis=1).astype(BF16)
    root = B_QK ** 0.5
    shift_a, shift_b = LANES - B_ROPE // 2, B_ROPE // 2
    pad_gain = lambda gn: jnp.pad(gn, (0, B_HEAD_PAD - B_QK))
    q_rows = _rope_gain_rows(pad_gain(q_gain), shift_a, shift_b, root * B_QSCALE)
    k_rows = _rope_gain_rows(pad_gain(k_gain), shift_a, shift_b, root)
    kvag = jnp.pad(kv_a_gain, (0, kv_in - B_KV_RANK))[None, :]
    seg = jnp.kron(jnp.eye(2, dtype=F32), jnp.ones((B_HEAD_PAD, B_HEAD_PAD), F32)).astype(BF16)
    row = lambda i: (i, 0)
    tab = pl.BlockSpec((tm, LANES), row)
    tiles = seq_len // tm
    return pl.pallas_call(
        _b_proj_body,
        grid=(t // tm,),
        in_specs=[
            pl.BlockSpec((tm, D_MODEL), row),
            _const_spec((1, D_MODEL)),
            _const_spec((D_MODEL, B_IN_PAD)),
            _const_spec((1, B_Q_RANK)),
            _const_spec((1, kv_in)),
            _const_spec((B_Q_RANK, B_HEADS * B_HEAD_PAD)),
            _const_spec((kv_in, B_HEADS * (B_HEAD_PAD + B_VDIM))),
            _const_spec((3, B_HEAD_PAD)),
            _const_spec((3, B_HEAD_PAD)),
            _const_spec((2 * B_HEAD_PAD, 2 * B_HEAD_PAD)),
            tab, tab, tab,
        ],
        out_specs=[
            pl.BlockSpec((tm, B_HEADS * B_HEAD_PAD), row),
            pl.BlockSpec((tm, B_HEADS * B_HEAD_PAD), row),
            pl.BlockSpec((1, tm // B_TK, B_HEADS * B_VDIM, B_TK), lambda i: (i // tiles, i % tiles, 0, 0)),
        ],
        out_shape=[
            jax.ShapeDtypeStruct((t, B_HEADS * B_HEAD_PAD), BF16),
            jax.ShapeDtypeStruct((t, B_HEADS * B_HEAD_PAD), BF16),
            jax.ShapeDtypeStruct((batch, seq_len // B_TK, B_HEADS * B_VDIM, B_TK), BF16),
        ],
        compiler_params=_params("parallel"),
        name="b_proj",
    )(x, gain[None, :], win, q_a_gain[None, :], kvag, wq, wkv_full, q_rows, k_rows, seg, cos, sin_a, sin_b)


def _b_attn_body(q_ref, k_ref, vt_ref, o_ref, m_ref, l_ref, acc_ref, ot_ref):
    seq_len = q_ref.shape[1]
    tq, tk = B_TQ, B_TK
    key = lax.broadcasted_iota(jnp.int32, (tk, tq), 0)
    qry = lax.broadcasted_iota(jnp.int32, (tk, tq), 1)
    diag_bias = jnp.where(key <= qry, 0.0, NEG_INF)
    n = seq_len // tq
    for kj in range(n):
        lo = kj * tq
        for hh in range(2):
            hl = slice(hh * LANES, (hh + 1) * LANES)
            k = k_ref[0, lo:lo + tk, hl]
            vt = vt_ref[0, kj, hh * B_VDIM:(hh + 1) * B_VDIM, :]
            vt1 = jnp.concatenate([vt, jnp.ones((2 * SUBLANES, tk), BF16)], axis=0)
            st = lax.dot_general(k, q_ref[0, lo:, hl], NT_DIMS, preferred_element_type=F32)
            diag = st[:, :tq] + diag_bias
            st = diag if kj == n - 1 else jnp.concatenate([diag, st[:, tq:]], axis=1)
            blk_max = jnp.max(st, axis=0, keepdims=True)
            if kj == 0:
                m_new = blk_max
                p = jnp.exp2(st - m_new)
                accx = jnp.dot(vt1, p.astype(BF16), preferred_element_type=F32)
                acc, l = accx[:B_VDIM], accx[B_VDIM:B_VDIM + 1]
            else:
                m = m_ref[hh, :, lo:]
                m_new = jnp.maximum(m, blk_max)
                alpha = jnp.exp2(m - m_new)
                p = jnp.exp2(st - m_new)
                accx = jnp.dot(vt1, p.astype(BF16), preferred_element_type=F32)
                l = alpha * l_ref[hh, :, lo:] + accx[B_VDIM:B_VDIM + 1]
                acc = alpha * acc_ref[hh, :, lo:] + accx[:B_VDIM]
            m_ref[hh, :, lo:] = m_new
            l_ref[hh, :, lo:] = l
            acc_ref[hh, :, lo:] = acc
            ot_ref[hh * B_VDIM:(hh + 1) * B_VDIM, :] = acc_ref[hh, :, lo:lo + tq] * (1.0 / l_ref[hh, :, lo:lo + tq])
        o_ref[0, lo:lo + tq, :] = ot_ref[...].T.astype(BF16)


def _b_attn(q, k, vt, batch, seq_len):
    return pl.pallas_call(
        _b_attn_body,
        grid=(batch, B_HEADS // 2),
        in_specs=[
            pl.BlockSpec((1, seq_len, 2 * B_HEAD_PAD), lambda b, hp: (b, 0, hp)),
            pl.BlockSpec((1, seq_len, 2 * B_HEAD_PAD), lambda b, hp: (b, 0, hp)),
            pl.BlockSpec((1, seq_len // B_TK, 2 * B_VDIM, B_TK), lambda b, hp: (b, 0, hp, 0)),
        ],
        out_specs=pl.BlockSpec((1, seq_len, 2 * B_VDIM), lambda b, hp: (b, 0, hp)),
        out_shape=jax.ShapeDtypeStruct((batch, seq_len, B_HEADS * B_VDIM), BF16),
        scratch_shapes=[
            pltpu.VMEM((2, 1, seq_len), F32),
            pltpu.VMEM((2, 1, seq_len), F32),
            pltpu.VMEM((2, B_VDIM, seq_len), F32),
            pltpu.VMEM((2 * B_VDIM, B_TQ), F32),
        ],
        compiler_params=_params("parallel", "parallel"),
        name="b_attn",
    )(q, k, vt)


def _mixer_b(x, gain, batch, seq_len, trig, w_in, q_a_gain, w_q_up, kv_a_gain, w_kv_up, q_gain, k_gain, w_out):
    t = x.shape[0]
    cos, sin = trig
    one = jnp.ones((t, B_NOPE), F32)
    zero64 = jnp.zeros((t, B_NOPE), F32)
    z16 = jnp.zeros_like(sin)
    tail = jnp.zeros((t, B_HEAD_PAD - B_QK), F32)
    tab_c = jnp.concatenate([one, cos, cos, tail + 1.0], axis=1)
    tab_sa = jnp.concatenate([zero64, -sin, z16, tail], axis=1)
    tab_sb = jnp.concatenate([zero64, z16, sin, tail], axis=1)
    q, k, vt = _b_proj(x, gain, batch, seq_len, w_in, q_a_gain, w_q_up, kv_a_gain, w_kv_up, q_gain, k_gain,
                       tab_c, tab_sa, tab_sb)
    shp = lambda a: a.reshape(batch, seq_len, a.shape[-1])
    o = _b_attn(shp(q), shp(k), vt, batch, seq_len)
    return o.reshape(t, B_HEADS * B_VDIM), w_out


C_TM = 512
C_QK_W = C_HEADS * C_KDIM
C_V_W = C_HEADS * C_VDIM
C_HPS = 4


def _c_proj_body(x_ref, g_ref, w_ref, c_ref, s_ref, o_ref, xn_ref):
    j = pl.program_id(1)
    width = w_ref.shape[1]

    @pl.when(j == 0)
    def _():
        xn_ref[...] = _rms(x_ref[...], g_ref[...]).astype(BF16)

    xn = xn_ref[...]

    @pl.when(j == 0)
    def _():
        cos, sin = c_ref[...], s_ref[...]
        for c in range(width // 256):
            y2 = jnp.dot(xn, w_ref[:, c * 256:(c + 1) * 256], preferred_element_type=F32)
            for hh in range(2):
                y = y2[:, hh * LANES:(hh + 1) * LANES]
                out = y * cos + pltpu.roll(y, C_KDIM // 2, 1) * sin
                lo = c * 256 + hh * LANES
                if lo >= C_QK_W:
                    out = out * (C_KDIM ** -0.5)
                o_ref[0, :, lo:lo + LANES] = out.astype(BF16)

    @pl.when(j == 1)
    def _():
        for c in range(width // 256):
            sl = slice(c * 256, (c + 1) * 256)
            o_ref[0, :, sl] = jnp.dot(xn, w_ref[:, sl], preferred_element_type=F32).astype(BF16)

    @pl.when(j == 2)
    def _():
        for c in range(width // 256):
            sl = slice(c * 256, (c + 1) * 256)
            gt = jnp.dot(xn, w_ref[:, sl], preferred_element_type=F32)
            o_ref[0, :, sl] = (gt * jax.nn.sigmoid(gt)).astype(BF16)


def _c_proj(x, gain, w_in, cos, sin):
    t = x.shape[0]
    tm = min(C_TM, t)
    width = 2 * C_QK_W
    tab = pl.BlockSpec((tm, LANES), lambda i, j: (i, 0))
    return pl.pallas_call(
        _c_proj_body,
        grid=(t // tm, 3),
        in_specs=[
            pl.BlockSpec((tm, D_MODEL), lambda i, j: (i, 0)),
            pl.BlockSpec((1, D_MODEL), lambda i, j: (0, 0)),
            pl.BlockSpec((D_MODEL, width), lambda i, j: (0, j)),
            tab, tab,
        ],
        out_specs=pl.BlockSpec((1, tm, width), lambda i, j: (j, i, 0)),
        out_shape=jax.ShapeDtypeStruct((3, t, width), BF16),
        scratch_shapes=[pltpu.VMEM((tm, D_MODEL), BF16)],
        compiler_params=_params("parallel", "arbitrary"),
        name="c_proj",
    )(x, gain[None, :], w_in.astype(BF16), cos, sin)


def _c_ret_body(q_ref, k_ref, v_ref, dec_ref, xi_ref, zeta_ref, cd_ref, y_ref, r_ref):
    n_chunks = q_ref.shape[2] // C_CHUNK
    r_ref[...] = jnp.zeros_like(r_ref)

    def step(n, carry):
        rows = pl.ds(pl.multiple_of(n * C_CHUNK, C_CHUNK), C_CHUNK)
        for hh in range(C_HPS):
            ks = slice(hh * C_KDIM, (hh + 1) * C_KDIM)
            vs = slice(hh * C_VDIM, (hh + 1) * C_VDIM)
            qc = q_ref[0, 0, rows, ks]
            kc = k_ref[0, 0, rows, ks]
            vc = v_ref[0, 0, rows, vs]
            s = lax.dot_general(qc, kc, NT_DIMS, preferred_element_type=F32) * dec_ref[hh]
            inner = jnp.dot(s.astype(BF16), vc, preferred_element_type=F32)
            r_old = r_ref[hh]
            cross = jnp.dot(qc, r_old.astype(BF16), preferred_element_type=F32) * xi_ref[hh]
            kz = (kc.astype(F32) * zeta_ref[hh]).astype(BF16)
            r_ref[hh] = cd_ref[hh] * r_old + lax.dot_general(kz, vc, TN_DIMS, preferred_element_type=F32)
            y = inner + cross
            mu = jnp.mean(y, axis=-1, keepdims=True)
            yc = y - mu
            var = jnp.mean(yc * yc, axis=-1, keepdims=True)
            y_ref[0, rows, vs] = (yc * lax.rsqrt(var + GN_EPS)).astype(BF16)
        return carry

    lax.fori_loop(0, n_chunks, step, 0)


def _c_ret(qkv, batch, seq_len):
    cc = C_CHUNK
    log_g = jnp.log(1.0 - 2.0 ** (-5.0 - jnp.arange(C_HEADS, dtype=F32)))
    idx = jnp.arange(cc, dtype=F32)
    diff = idx[:, None] - idx[None, :]
    causal = diff >= 0
    decay = jnp.where(causal[None], jnp.exp(jnp.where(causal, diff, 0.0)[None] * log_g[:, None, None]), 0.0)
    xi = jnp.exp((idx + 1.0)[None, :] * log_g[:, None])[:, :, None]
    zeta = jnp.exp((cc - 1.0 - idx)[None, :] * log_g[:, None])[:, :, None]
    cdec = jnp.broadcast_to(jnp.exp(cc * log_g)[:, None, None], (C_HEADS, 1, C_VDIM))
    groups = C_HEADS // C_HPS
    per_head = lambda shape: pl.BlockSpec((C_HPS,) + shape, lambda b, h: (h, 0, 0))
    return pl.pallas_call(
        _c_ret_body,
        grid=(batch, groups),
        in_specs=[
            pl.BlockSpec((1, 1, seq_len, C_HPS * C_KDIM), lambda b, h: (0, b, 0, h)),
            pl.BlockSpec((1, 1, seq_len, C_HPS * C_KDIM), lambda b, h: (0, b, 0, groups + h)),
            pl.BlockSpec((1, 1, seq_len, C_HPS * C_VDIM), lambda b, h: (1, b, 0, h)),
            per_head((cc, cc)), per_head((cc, 1)), per_head((cc, 1)), per_head((1, C_VDIM)),
        ],
        out_specs=pl.BlockSpec((1, seq_len, C_HPS * C_VDIM), lambda b, h: (b, 0, h)),
        out_shape=jax.ShapeDtypeStruct((batch, seq_len, C_V_W), BF16),
        scratch_shapes=[pltpu.VMEM((C_HPS, C_KDIM, C_VDIM), F32)],
        compiler_params=_params("parallel", "parallel"),
        name="c_ret",
    )(qkv, qkv, qkv, decay, xi, zeta, cdec)


def _mixer_c(x, gain, batch, seq_len, trig, w_in, w_out):
    t = x.shape[0]
    cos, sin = trig
    tab_c = jnp.concatenate([cos, cos], axis=1)
    tab_s = jnp.concatenate([-sin, sin], axis=1)
    proj = _c_proj(x, gain, w_in, tab_c, tab_s)
    y = _c_ret(proj.reshape(3, batch, seq_len, 2 * C_QK_W), batch, seq_len)
    return y.reshape(t, C_V_W), proj, w_out


D_TS = 256


def _d_body(x_ref, g_ref, win_ref, cw_ref, cb_ref, wrg_ref, brg_ref, wig_ref, big_ref, lru_ref,
            o_ref, ubuf_ref, a_ref, b_ref, hs_ref, h_ref):
    ti = pl.program_id(1)
    ts = x_ref.shape[1]
    pad = SUBLANES

    @pl.when(ti == 0)
    def _():
        ubuf_ref[0:pad, :] = jnp.zeros((pad, D_WIDTH), F32)
        h_ref[...] = jnp.zeros_like(h_ref)

    xn = _rms(x_ref[0], g_ref[...]).astype(BF16)
    gate = jnp.dot(xn, win_ref[:, :D_WIDTH], preferred_element_type=F32)
    u = jnp.dot(xn, win_ref[:, D_WIDTH:], preferred_element_type=F32)
    ubuf_ref[pad:pad + ts, :] = u
    uc = cb_ref[...] + cw_ref[D_CONV - 1:D_CONV, :] * u
    for k in range(D_CONV - 1):
        off = pad - (D_CONV - 1) + k
        uc = uc + cw_ref[k:k + 1, :] * ubuf_ref[off:off + ts, :]
    ubuf_ref[0:pad, :] = ubuf_ref[ts:ts + pad, :]

    ucb = uc.astype(BF16)
    softplus_neg = jax.nn.softplus(-lru_ref[...])
    for n in range(D_BLOCKS):
        sl = slice(n * D_BLOCK, (n + 1) * D_BLOCK)
        r = jax.nn.sigmoid(jnp.dot(ucb[:, sl], wrg_ref[n], preferred_element_type=F32) + brg_ref[:, sl])
        ig = jax.nn.sigmoid(jnp.dot(ucb[:, sl], wig_ref[n], preferred_element_type=F32) + big_ref[:, sl])
        log_a = -LRU_C * r * softplus_neg[:, sl]
        a = jnp.exp(log_a)
        a_ref[:, sl] = a
        b_ref[:, sl] = jnp.sqrt(jnp.maximum(-jnp.tanh(log_a) * (1.0 + a * a), 0.0)) * (ig * uc[:, sl])

    row = lax.broadcasted_iota(jnp.int32, (SUBLANES, D_WIDTH), 0)

    def scan_group(j, h):
        rows = pl.ds(pl.multiple_of(j * SUBLANES, SUBLANES), SUBLANES)
        a = a_ref[rows, :]
        b = b_ref[rows, :]
        for s in (1, 2, 4):
            keep = row >= s
            b = jnp.where(keep, a * pltpu.roll(b, s, 0) + b, b)
            a = jnp.where(keep, a * pltpu.roll(a, s, 0), a)
        hs = a * h + b
        hs_ref[rows, :] = hs
        return jnp.broadcast_to(hs[SUBLANES - 1:SUBLANES, :], (SUBLANES, D_WIDTH))

    h_ref[...] = lax.fori_loop(0, ts // SUBLANES, scan_group, h_ref[...])
    o_ref[0] = (jax.nn.gelu(gate) * hs_ref[...]).astype(BF16)


def _d_main(x3, gain, w_in, conv_w, conv_b, w_rg, b_rg, w_ig, b_ig, lru_param):
    batch, seq_len, _ = x3.shape
    ts = min(D_TS, seq_len)
    vec = lambda a: a[None, :]
    return pl.pallas_call(
        _d_body,
        grid=(batch, seq_len // ts),
        in_specs=[
            pl.BlockSpec((1, ts, D_MODEL), lambda b, i: (b, i, 0)),
            _const_spec((1, D_MODEL)),
            _const_spec((D_MODEL, 2 * D_WIDTH)),
            _const_spec((D_CONV, D_WIDTH)),
            _const_spec((1, D_WIDTH)),
            _const_spec((D_BLOCKS, D_BLOCK, D_BLOCK)),
            _const_spec((1, D_WIDTH)),
            _const_spec((D_BLOCKS, D_BLOCK, D_BLOCK)),
            _const_spec((1, D_WIDTH)),
            _const_spec((1, D_WIDTH)),
        ],
        out_specs=pl.BlockSpec((1, ts, D_WIDTH), lambda b, i: (b, i, 0)),
        out_shape=jax.ShapeDtypeStruct((batch, seq_len, D_WIDTH), BF16),
        scratch_shapes=[
            pltpu.VMEM((ts + SUBLANES, D_WIDTH), F32),
            pltpu.VMEM((ts, D_WIDTH), F32),
            pltpu.VMEM((ts, D_WIDTH), F32),
            pltpu.VMEM((ts, D_WIDTH), F32),
            pltpu.VMEM((SUBLANES, D_WIDTH), F32),
        ],
        compiler_params=_params("parallel", "arbitrary"),
        name="d_main",
    )(x3, vec(gain), w_in.astype(BF16), conv_w, vec(conv_b), w_rg.astype(BF16), vec(b_rg),
      w_ig.astype(BF16), vec(b_ig), vec(lru_param))


def _mixer_d(x, gain, batch, seq_len, w_in, conv_w, conv_b, w_rg, b_rg, w_ig, b_ig, lru_param, w_out):
    t = x.shape[0]
    y = _d_main(x.reshape(batch, seq_len, D_MODEL), gain, w_in, conv_w, conv_b, w_rg, b_rg, w_ig, b_ig, lru_param)
    return y.reshape(t, D_WIDTH), w_out


def kernel(x, positions, norm_gains, ffn_w_in, ffn_w_out, a_w_in, a_q_gain, a_k_gain, a_w_out, b_w_in, b_q_a_gain, b_w_q_up, b_kv_a_gain, b_w_kv_up, b_q_gain, b_k_gain, b_w_out, c_w_in, c_w_out, d_w_in, d_conv_w, d_conv_b, d_w_rg, d_b_rg, d_w_ig, d_b_ig, d_lru_param, d_w_out):
    batch, seq_len, _ = x.shape
    depth = norm_gains.shape[0]
    t = batch * seq_len
    h = x.reshape(t, D_MODEL)
    trig_a = _rope_trig(positions, A_HEAD_DIM // 2)
    trig_b = _rope_trig(positions, B_ROPE // 2)
    trig_c = _rope_trig(positions, C_KDIM // 2)
    for i in range(depth):
        m, j = i % 4, i // 4
        h = _ffn(h, norm_gains[i, 0], ffn_w_in[i, 0], ffn_w_out[i, 0])
        g = norm_gains[i, 1]
        pre = None
        if m == 0:
            h = _mixer_a(h, g, batch, seq_len, trig_a, a_w_in[j], a_q_gain[j], a_k_gain[j], a_w_out[j])
        elif m == 1:
            pre = _mixer_b(h, g, batch, seq_len, trig_b, b_w_in[j], b_q_a_gain[j], b_w_q_up[j], b_kv_a_gain[j],
                           b_w_kv_up[j], b_q_gain[j], b_k_gain[j], b_w_out[j])
        elif m == 2:
            pre = _mixer_c(h, g, batch, seq_len, trig_c, c_w_in[j], c_w_out[j])
        else:
            pre = _mixer_d(h, g, batch, seq_len, d_w_in[j], d_conv_w[j], d_conv_b[j], d_w_rg[j], d_b_rg[j],
                           d_w_ig[j], d_b_ig[j], d_lru_param[j], d_w_out[j])
        h = _ffn(h, norm_gains[i, 2], ffn_w_in[i, 1], ffn_w_out[i, 1], pre=pre)
    return h.reshape(batch, seq_len, D_MODEL)
```
